```python
import jax, jax.numpy as jnp
from jax import lax
import numpy as np

D_MODEL = 1024
BATCH = 4
SEQ = 4096
DEPTH = 2

N_MIXERS = 2
N_CONV_LAYERS = (DEPTH + 1) // 2
N_ATTN_LAYERS = DEPTH // 2
CONV_WIDTH = 3
N_HEADS = 16
HEAD_DIM = D_MODEL // N_HEADS
GRID_W = 64
WIN_H_MAX = 8
WIN_W = 16
D_FF = 4 * D_MODEL
NORM_EPS = 1e-6

kernel_name = "hybrid_shortconv_natten2d_encoder"


def rmsnorm(x, g):
    xf = x.astype(jnp.float32)
    y = xf * lax.rsqrt(jnp.mean(xf * xf, axis=-1, keepdims=True) + NORM_EPS)
    return (y * g.astype(jnp.float32)).astype(x.dtype)


def short_conv_mixer(h, w_in, conv_w, w_out):
    u = h @ w_in
    gate_b, gate_c, v = jnp.split(u, 3, axis=-1)
    z = gate_c * v
    zp = jnp.pad(z, ((0, 0), (1, 1), (0, 0)))
    zc = conv_w[0] * zp[:, :-2] + conv_w[1] * zp[:, 1:-1] + conv_w[2] * zp[:, 2:]
    return (gate_b * zc) @ w_out


def neighborhood_attention_2d(h, w_qkv, rpb, w_o):
    bsz, seq, d = h.shape
    rows = seq // GRID_W
    kh = min(WIN_H_MAX, rows)
    qkv = (h @ w_qkv).reshape(bsz, rows, GRID_W, 3, N_HEADS, HEAD_DIM)
    q = qkv[:, :, :, 0] * (HEAD_DIM ** -0.5)
    k = qkv[:, :, :, 1]
    v = qkv[:, :, :, 2]

    cols = jnp.arange(GRID_W)
    row_start = jnp.clip(jnp.arange(rows) - kh // 2, 0, rows - kh)
    col_start = jnp.clip(cols - WIN_W // 2, 0, GRID_W - WIN_W)
    col_idx = col_start[:, None] + jnp.arange(WIN_W)[None, :]
    col_off = col_idx - cols[:, None] + (WIN_W - 1)
    rpb_cols = rpb[:, :, col_off]

    def row_block(args):
        q_row, r = args
        s = row_start[r]
        k_rows = lax.dynamic_slice_in_dim(k, s, kh, axis=1)
        v_rows = lax.dynamic_slice_in_dim(v, s, kh, axis=1)
        k_nb = k_rows[:, :, col_idx]
        v_nb = v_rows[:, :, col_idx]
        scores = jnp.einsum('bqhd,brqchd->bhqrc', q_row, k_nb).astype(jnp.float32)
        row_off = s + jnp.arange(kh) - r + (WIN_H_MAX - 1)
        bias = jnp.transpose(rpb_cols[:, row_off], (0, 2, 1, 3))
        scores = scores + bias[None].astype(jnp.float32)
        p = jax.nn.softmax(scores.reshape(bsz, N_HEADS, GRID_W, kh * WIN_W), axis=-1)
        p = p.reshape(bsz, N_HEADS, GRID_W, kh, WIN_W).astype(v.dtype)
        return jnp.einsum('bhqrc,brqchd->bqhd', p, v_nb)

    out = lax.map(row_block, (jnp.transpose(q, (1, 0, 2, 3, 4)), jnp.arange(rows)))
    out = jnp.transpose(out, (1, 0, 2, 3, 4)).reshape(bsz, seq, d)
    return out @ w_o


def sqrelu_mlp(h, w_up, w_down):
    a = jax.nn.relu(h @ w_up)
    return (a * a) @ w_down


def setup_inputs(seed: int = 0) -> dict:
    key = jax.random.key(seed)
    ks = jax.random.split(key, 16)
    d = D_MODEL
    f32 = jnp.float32
    nrm = lambda k, shp, s: jax.random.normal(k, shp, f32) * s
    return {
        "x": nrm(ks[0], (BATCH, SEQ, d), 1.0),
        "norm_mix": 1.0 + nrm(ks[1], (DEPTH, d), 0.05),
        "conv_w_in": nrm(ks[2], (N_CONV_LAYERS, d, 3 * d), d ** -0.5),
        "conv_w": nrm(ks[3], (N_CONV_LAYERS, CONV_WIDTH, d), CONV_WIDTH ** -0.5),
        "conv_w_out": nrm(ks[4], (N_CONV_LAYERS, d, d), d ** -0.5),
        "attn_w_qkv": nrm(ks[5], (N_ATTN_LAYERS, d, 3 * d), d ** -0.5),
        "attn_rpb": nrm(ks[6], (N_ATTN_LAYERS, N_HEADS, 2 * WIN_H_MAX - 1, 2 * WIN_W - 1), 0.5),
        "attn_w_o": nrm(ks[7], (N_ATTN_LAYERS, d, d), d ** -0.5),
        "norm_mlp": 1.0 + nrm(ks[8], (DEPTH, d), 0.05),
        "mlp_w_up": nrm(ks[9], (DEPTH, d, D_FF), d ** -0.5),
        "mlp_w_down": nrm(ks[10], (DEPTH, D_FF, d), D_FF ** -0.5),
        "norm_final": 1.0 + nrm(ks[11], (d,), 0.05),
    }


def reference(x, norm_mix, conv_w_in, conv_w, conv_w_out, attn_w_qkv, attn_rpb,
              attn_w_o, norm_mlp, mlp_w_up, mlp_w_down, norm_final):
    for i in range(DEPTH):
        h = rmsnorm(x, norm_mix[i])
        j = i // N_MIXERS
        if i % N_MIXERS == 0:
            x = x + short_conv_mixer(h, conv_w_in[j], conv_w[j], conv_w_out[j])
        else:
            x = x + neighborhood_attention_2d(h, attn_w_qkv[j], attn_rpb[j], attn_w_o[j])
        h = rmsnorm(x, norm_mlp[i])
        x = x + sqrelu_mlp(h, mlp_w_up[i], mlp_w_down[i])
    return rmsnorm(x, norm_final)
```

```python
import functools

import jax
import jax.numpy as jnp
from jax import lax
from jax.experimental import pallas as pl
from jax.experimental.pallas import tpu as pltpu

F32 = jnp.float32
BF16 = jnp.bfloat16

NORM_EPS = 1e-6
N_HEADS = 16
HEAD_DIM = 64
GRID_W = 64
WIN_H = 8
WIN_W = 16
LANES = 128
N_PAIRS = N_HEADS * HEAD_DIM // LANES

ROW_TILE = 512
FF_CHUNK = 1024
Q_ROWS = 4
KEY_BLOCK_ROWS = 4
N_KEY_BLOCKS = Q_ROWS // KEY_BLOCK_ROWS + WIN_H // KEY_BLOCK_ROWS
MASKED = -1e30

N_FULL = 2 * WIN_H - 2
E_LOW_MASKED = N_FULL
E_HIGH_MASKED = N_FULL + 1
N_ENTRIES = N_FULL + 2

VMEM_LIMIT = 52 * 1024 * 1024


def _params():
    return pltpu.CompilerParams(dimension_semantics=("arbitrary",), vmem_limit_bytes=VMEM_LIMIT)


def _resident(shape):
    return pl.BlockSpec(shape, lambda *_: (0,) * len(shape), pipeline_mode=pl.Buffered(1))


def _rmsnorm(x, g):
    ms = jnp.mean(x * x, axis=-1, keepdims=True)
    return x * lax.rsqrt(ms + NORM_EPS) * g


def _conv_in_kernel(x_ref, g_ref, w_ref, gb_ref, z_ref):
    d = x_ref.shape[1]
    h = _rmsnorm(x_ref[...], g_ref[...]).astype(BF16)
    gb_ref[...] = jnp.dot(h, w_ref[:, 0:d], preferred_element_type=F32)
    c = jnp.dot(h, w_ref[:, d:2 * d], preferred_element_type=F32)
    v = jnp.dot(h, w_ref[:, 2 * d:3 * d], preferred_element_type=F32)
    z_ref[...] = c * v


def _conv_in(x, g, w):
    n, d = x.shape
    tile = pl.BlockSpec((ROW_TILE, d), lambda i: (i, 0))
    return pl.pallas_call(
        _conv_in_kernel,
        grid=(n // ROW_TILE,),
        in_specs=[tile, _resident((1, d)), _resident((d, 3 * d))],
        out_specs=[tile, tile],
        out_shape=[jax.ShapeDtypeStruct((n, d), F32)] * 2,
        compiler_params=_params(),
        name="conv_in",
    )(x, g, w)


def _conv_out_kernel(z_ref, zprev_ref, znext_ref, gb_ref, cw_ref, x_ref, w_ref, o_ref, *, tiles_per_seq):
    tm = z_ref.shape[0]
    pos = lax.rem(pl.program_id(0), tiles_per_seq)
    z = z_ref[...]
    before = jnp.where(pos == 0, 0.0, zprev_ref[7:8, :])
    after = jnp.where(pos == tiles_per_seq - 1, 0.0, znext_ref[0:1, :])
    row = lax.broadcasted_iota(jnp.int32, (tm, 1), 0)
    z_m1 = jnp.where(row == 0, before, pltpu.roll(z, 1, axis=0))
    z_p1 = jnp.where(row == tm - 1, after, pltpu.roll(z, tm - 1, axis=0))
    zc = cw_ref[0:1, :] * z_m1 + cw_ref[1:2, :] * z + cw_ref[2:3, :] * z_p1
    y = (gb_ref[...] * zc).astype(BF16)
    o_ref[...] = x_ref[...] + jnp.dot(y, w_ref[...], preferred_element_type=F32)


def _conv_out(z, gb, cw, x, w, seq):
    n, d = x.shape
    halo = 8
    per = ROW_TILE // halo
    tile = pl.BlockSpec((ROW_TILE, d), lambda i: (i, 0))
    prev_spec = pl.BlockSpec((halo, d), lambda i: (jnp.maximum(i * per - 1, 0), 0))
    next_spec = pl.BlockSpec((halo, d), lambda i: (jnp.minimum((i + 1) * per, n // halo - 1), 0))
    return pl.pallas_call(
        functools.partial(_conv_out_kernel, tiles_per_seq=seq // ROW_TILE),
        grid=(n // ROW_TILE,),
        in_specs=[tile, prev_spec, next_spec, tile, _resident(cw.shape), tile, _resident((d, d))],
        out_specs=tile,
        out_shape=jax.ShapeDtypeStruct((n, d), F32),
        compiler_params=_params(),
        name="conv_out",
    )(z, z, z, gb, cw, x, w)


def _mlp_kernel(x_ref, g_ref, wup_ref, wdn_ref, *rest, final_norm):
    o_ref = rest[-1]
    ff = wup_ref.shape[1]
    x = x_ref[...]
    h = _rmsnorm(x, g_ref[...]).astype(BF16)
    acc = x
    for c in range(ff // FF_CHUNK):
        cols = slice(c * FF_CHUNK, (c + 1) * FF_CHUNK)
        a = jnp.maximum(jnp.dot(h, wup_ref[:, cols], preferred_element_type=F32), 0.0)
        acc = acc + jnp.dot((a * a).astype(BF16), wdn_ref[cols, :], preferred_element_type=F32)
    if final_norm:
        acc = _rmsnorm(acc, rest[0][...])
    o_ref[...] = acc


def _mlp(x, g, wup, wdn, g_final=None):
    n, d = x.shape
    ff = wup.shape[1]
    tile = pl.BlockSpec((ROW_TILE, d), lambda i: (i, 0))
    in_specs = [tile, _resident((1, d)), _resident((d, ff)), _resident((ff, d))]
    args = [x, g, wup, wdn]
    if g_final is not None:
        in_specs.append(_resident((1, d)))
        args.append(g_final)
    return pl.pallas_call(
        functools.partial(_mlp_kernel, final_norm=g_final is not None),
        grid=(n // ROW_TILE,),
        in_specs=in_specs,
        out_specs=tile,
        out_shape=jax.ShapeDtypeStruct((n, d), F32),
        compiler_params=_params(),
        name="mlp_final" if g_final is not None else "mlp",
    )(*args)


def _qkv_kernel(x_ref, g_ref, w_ref, o_ref):
    d = x_ref.shape[1]
    h = _rmsnorm(x_ref[...], g_ref[...]).astype(BF16)
    for t in range(3):
        u = jnp.dot(h, w_ref[:, t * d:(t + 1) * d], preferred_element_type=F32)
        if t == 0:
            u = u * (HEAD_DIM ** -0.5)
        u = u.astype(BF16)
        for p in range(N_PAIRS):
            o_ref[t, p] = u[:, p * LANES:(p + 1) * LANES]


def _qkv(x, g, w):
    n, d = x.shape
    return pl.pallas_call(
        _qkv_kernel,
        grid=(n // ROW_TILE,),
        in_specs=[pl.BlockSpec((ROW_TILE, d), lambda i: (i, 0)), _resident((1, d)), _resident((d, 3 * d))],
        out_specs=pl.BlockSpec((3, N_PAIRS, ROW_TILE, LANES), lambda i: (0, 0, i, 0)),
        out_shape=jax.ShapeDtypeStruct((3, N_PAIRS, n, LANES), BF16),
        compiler_params=_params(),
        name="qkv",
    )(x, g, w)


def _bias_table_kernel(rpb_ref, o_ref):
    shape = (GRID_W, LANES)
    c = lax.broadcasted_iota(jnp.int32, shape, 0)
    lane = lax.broadcasted_iota(jnp.int32, shape, 1)
    kc = lane & (GRID_W - 1)
    col_start = jnp.clip(c - WIN_W // 2, 0, GRID_W - WIN_W)
    in_window = (kc >= col_start) & (kc < col_start + WIN_W)
    left = lane < GRID_W
    tiles = []
    for r in range(2 * WIN_H - 1):
        row = jnp.broadcast_to(rpb_ref[0, r:r + 1, :], shape)
        t_left = pltpu.roll(row, LANES - (WIN_W - 1), axis=1, stride=1, stride_axis=0)
        t_right = pltpu.roll(row, GRID_W - (WIN_W - 1), axis=1, stride=1, stride_axis=0)
        tiles.append(jnp.where(in_window, jnp.where(left, t_left, t_right), MASKED))
    for e in range(N_FULL):
        o_ref[0, e] = jnp.where(left, tiles[e], tiles[e + 1])
    o_ref[0, E_LOW_MASKED] = jnp.where(left, MASKED, tiles[WIN_H - 1 - WIN_H // 2])
    o_ref[0, E_HIGH_MASKED] = jnp.where(left, tiles[WIN_H - 1 + WIN_H // 2 - 1], MASKED)


def _bias_table(rpb):
    h, nr, nc = rpb.shape
    padded = jnp.pad(rpb, ((0, 0), (0, 16 - nr), (0, LANES - nc)))
    return pl.pallas_call(
        _bias_table_kernel,
        grid=(h,),
        in_specs=[pl.BlockSpec((1, 16, LANES), lambda i: (i, 0, 0))],
        out_specs=pl.BlockSpec((1, N_ENTRIES, GRID_W, LANES), lambda i: (i, 0, 0, 0)),
        out_shape=jax.ShapeDtypeStruct((h, N_ENTRIES, GRID_W, LANES), F32),
        compiler_params=_params(),
        name="bias_table",
    )(padded)


def _window_tiles(case, rq):
    offset = (0, WIN_H // 2, WIN_H)[case]
    first = (0, rq, WIN_H // 2)[case]
    out = []
    for p in range(N_KEY_BLOCKS * KEY_BLOCK_ROWS // 2):
        lo_ok = first <= 2 * p < first + WIN_H
        hi_ok = first <= 2 * p + 1 < first + WIN_H
        dr = 2 * p - rq - offset
        if lo_ok and hi_ok:
            out.append((p, dr + WIN_H - 1))
        elif hi_ok:
            assert dr == -WIN_H // 2 - 1
            out.append((p, E_LOW_MASKED))
        elif lo_ok:
            assert dr == WIN_H // 2 - 1
            out.append((p, E_HIGH_MASKED))
    return out


def _attn_kernel(q_ref, k0_ref, k1_ref, k2_ref, v0_ref, v1_ref, v2_ref, tbl_ref, x_ref, wo_ref, o_ref, obuf):
    mq = q_ref.shape[1]
    step = pl.program_id(1)
    last = pl.num_programs(1) - 1
    n_key_tiles = N_KEY_BLOCKS * KEY_BLOCK_ROWS // 2
    lane = lax.broadcasted_iota(jnp.int32, (mq, LANES), 1)
    left = lane < HEAD_DIM

    def pair_body(case, p, carry):
        q = q_ref[p]
        q2 = jnp.concatenate([jnp.where(left, q, 0), jnp.where(left, 0, q)], axis=0)
        k = jnp.concatenate([k0_ref[p], k1_ref[p], k2_ref[p]], axis=0)
        v = jnp.concatenate([v0_ref[p], v1_ref[p], v2_ref[p]], axis=0)
        s = lax.dot_general(q2, k, (((1,), (1,)), ((), ())), preferred_element_type=F32)
        prob_rows, denom_rows = [], []
        for hh in range(2):
            head = 2 * p + hh
            for rq in range(Q_ROWS):
                r0 = hh * mq + rq * GRID_W
                tiles = _window_tiles(case, rq)
                st = [s[r0:r0 + GRID_W, t * LANES:(t + 1) * LANES] + tbl_ref[head, e] for t, e in tiles]
                m = functools.reduce(jnp.maximum, st)
                m = jnp.max(m, axis=1, keepdims=True)
                ex = [jnp.exp(t - m) for t in st]
                denom_rows.append(jnp.sum(functools.reduce(jnp.add, ex), axis=1, keepdims=True))
                by_tile = {t: x for (t, _), x in zip(tiles, ex)}
                prob_rows.append(jnp.concatenate(
                    [by_tile[t].astype(BF16) if t in by_tile else jnp.zeros((GRID_W, LANES), BF16)
                     for t in range(n_key_tiles)], axis=1))
        probs = jnp.concatenate(prob_rows, axis=0)
        o = jnp.dot(probs, v, preferred_element_type=F32) / jnp.concatenate(denom_rows, axis=0)
        obuf[p] = jnp.where(left, o[:mq], o[mq:]).astype(BF16)
        return carry

    for case, cond in ((0, step == 0), (1, (step > 0) & (step < last)), (2, step == last)):
        @pl.when(cond)
        def _(case=case):
            lax.fori_loop(0, N_PAIRS, functools.partial(pair_body, case), 0)

    a = jnp.concatenate([obuf[p] for p in range(N_PAIRS)], axis=1)
    o_ref[...] = x_ref[...] + jnp.dot(a, wo_ref[...], preferred_element_type=F32)


def _attn(qkv, tbl, x, wo, batch, seq):
    n, d = x.shape
    mq = Q_ROWS * GRID_W
    kb = KEY_BLOCK_ROWS * GRID_W
    steps = seq // mq
    kb_per_seq = seq // kb
    last_start = kb_per_seq - N_KEY_BLOCKS

    def kv_spec(t, j):
        def index(b, i):
            return (t, 0, b * kb_per_seq + jnp.clip(i - 1, 0, last_start) + j, 0)
        return pl.BlockSpec((None, N_PAIRS, kb, LANES), index)

    tile = pl.BlockSpec((mq, d), lambda b, i: (b * steps + i, 0))
    in_specs = [pl.BlockSpec((None, N_PAIRS, mq, LANES), lambda b, i: (0, 0, b * steps + i, 0))]
    in_specs += [kv_spec(1, j) for j in range(N_KEY_BLOCKS)]
    in_specs += [kv_spec(2, j) for j in range(N_KEY_BLOCKS)]
    in_specs += [_resident(tbl.shape), tile, _resident((d, d))]
    return pl.pallas_call(
        _attn_kernel,
        grid=(batch, steps),
        in_specs=in_specs,
        out_specs=tile,
        out_shape=jax.ShapeDtypeStruct((n, d), F32),
        scratch_shapes=[pltpu.VMEM((N_PAIRS, mq, LANES), BF16)],
        compiler_params=pltpu.CompilerParams(
            dimension_semantics=("arbitrary", "arbitrary"), vmem_limit_bytes=VMEM_LIMIT),
        name="natten",
    )(*([qkv] * (1 + 2 * N_KEY_BLOCKS)), tbl, x, wo)


def kernel(x, norm_mix, conv_w_in, conv_w, conv_w_out, attn_w_qkv, attn_rpb, attn_w_o, norm_mlp, mlp_w_up,
           mlp_w_down, norm_final):
    batch, seq, d = x.shape
    n = batch * seq
    assert d == N_HEADS * HEAD_DIM and seq % (GRID_W * Q_ROWS) == 0 and seq // GRID_W >= WIN_H + Q_ROWS
    assert seq % ROW_TILE == 0 and N_KEY_BLOCKS == 3 and Q_ROWS == KEY_BLOCK_ROWS == WIN_H // 2
    assert norm_mix.shape[0] == 2 and conv_w_in.shape[0] == 1 and attn_w_qkv.shape[0] == 1

    xf = x.reshape(n, d)
    gb, z = _conv_in(xf, norm_mix[0:1], conv_w_in[0].astype(BF16))
    xf = _conv_out(z, gb, conv_w[0], xf, conv_w_out[0].astype(BF16), seq)
    xf = _mlp(xf, norm_mlp[0:1], mlp_w_up[0].astype(BF16), mlp_w_down[0].astype(BF16))

    qkv = _qkv(xf, norm_mix[1:2], attn_w_qkv[0].astype(BF16))
    tbl = _bias_table(attn_rpb[0])
    xf = _attn(qkv, tbl, xf, attn_w_o[0].astype(BF16), batch, seq)
    out = _mlp(xf, norm_mlp[1:2], mlp_w_up[1].astype(BF16), mlp_w_down[1].astype(BF16), norm_final.reshape(1, d))
    return out.reshape(batch, seq, d)
```

```python
import functools

import jax
import jax.numpy as jnp
from jax import lax
from jax.experimental import pallas as pl
from jax.experimental.pallas import tpu as pltpu

F32 = jnp.float32
BF16 = jnp.bfloat16

NORM_EPS = 1e-6
N_HEADS = 16
HEAD_DIM = 64
GRID_W = 64
WIN_H = 8
WIN_W = 16
LANES = 128
N_PAIRS = N_HEADS * HEAD_DIM // LANES

ROW_TILE = 512
FF_CHUNK = 1024
Q_ROWS = 4
KEY_BLOCK_ROWS = 4
N_KEY_BLOCKS = Q_ROWS // KEY_BLOCK_ROWS + WIN_H // KEY_BLOCK_ROWS
MASKED = -1e30
PAIR_UNROLL = 8

N_FULL = 2 * WIN_H - 2
E_LOW_MASKED = N_FULL
E_HIGH_MASKED = N_FULL + 1
N_ENTRIES = N_FULL + 2

VMEM_LIMIT = 52 * 1024 * 1024


def _params():
    return pltpu.CompilerParams(dimension_semantics=("arbitrary",), vmem_limit_bytes=VMEM_LIMIT)


def _resident(shape):
    return pl.BlockSpec(shape, lambda *_: (0,) * len(shape), pipeline_mode=pl.Buffered(1))


def _rmsnorm(x, g):
    ms = jnp.mean(x * x, axis=-1, keepdims=True)
    return x * lax.rsqrt(ms + NORM_EPS) * g


def _conv_in_kernel(x_ref, g_ref, w_ref, gb_ref, z_ref):
    d = x_ref.shape[1]
    h = _rmsnorm(x_ref[...], g_ref[...]).astype(BF16)
    gb_ref[...] = jnp.dot(h, w_ref[:, 0:d], preferred_element_type=F32)
    c = jnp.dot(h, w_ref[:, d:2 * d], preferred_element_type=F32)
    v = jnp.dot(h, w_ref[:, 2 * d:3 * d], preferred_element_type=F32)
    z_ref[...] = c * v


def _conv_in(x, g, w):
    n, d = x.shape
    tile = pl.BlockSpec((ROW_TILE, d), lambda i: (i, 0))
    return pl.pallas_call(
        _conv_in_kernel,
        grid=(n // ROW_TILE,),
        in_specs=[tile, _resident((1, d)), _resident((d, 3 * d))],
        out_specs=[tile, tile],
        out_shape=[jax.ShapeDtypeStruct((n, d), F32)] * 2,
        compiler_params=_params(),
        name="conv_in",
    )(x, g, w)


def _conv_out_kernel(z_ref, zprev_ref, znext_ref, gb_ref, cw_ref, x_ref, w_ref, o_ref, *, tiles_per_seq):
    tm = z_ref.shape[0]
    pos = lax.rem(pl.program_id(0), tiles_per_seq)
    z = z_ref[...]
    before = jnp.where(pos == 0, 0.0, zprev_ref[7:8, :])
    after = jnp.where(pos == tiles_per_seq - 1, 0.0, znext_ref[0:1, :])
    row = lax.broadcasted_iota(jnp.int32, (tm, 1), 0)
    z_m1 = jnp.where(row == 0, before, pltpu.roll(z, 1, axis=0))
    z_p1 = jnp.where(row == tm - 1, after, pltpu.roll(z, tm - 1, axis=0))
    zc = cw_ref[0:1, :] * z_m1 + cw_ref[1:2, :] * z + cw_ref[2:3, :] * z_p1
    y = (gb_ref[...] * zc).astype(BF16)
    o_ref[...] = x_ref[...] + jnp.dot(y, w_ref[...], preferred_element_type=F32)


def _conv_out(z, gb, cw, x, w, seq):
    n, d = x.shape
    halo = 8
    per = ROW_TILE // halo
    tile = pl.BlockSpec((ROW_TILE, d), lambda i: (i, 0))
    prev_spec = pl.BlockSpec((halo, d), lambda i: (jnp.maximum(i * per - 1, 0), 0))
    next_spec = pl.BlockSpec((halo, d), lambda i: (jnp.minimum((i + 1) * per, n // halo - 1), 0))
    return pl.pallas_call(
        functools.partial(_conv_out_kernel, tiles_per_seq=seq // ROW_TILE),
        grid=(n // ROW_TILE,),
        in_specs=[tile, prev_spec, next_spec, tile, _resident(cw.shape), tile, _resident((d, d))],
        out_specs=tile,
        out_shape=jax.ShapeDtypeStruct((n, d), F32),
        compiler_params=_params(),
        name="conv_out",
    )(z, z, z, gb, cw, x, w)


def _mlp_kernel(x_ref, g_ref, wup_ref, wdn_ref, *rest, final_norm):
    o_ref = rest[-1]
    ff = wup_ref.shape[1]
    x = x_ref[...]
    h = _rmsnorm(x, g_ref[...]).astype(BF16)
    acc = x
    for c in range(ff // FF_CHUNK):
        cols = slice(c * FF_CHUNK, (c + 1) * FF_CHUNK)
        a = jnp.maximum(jnp.dot(h, wup_ref[:, cols], preferred_element_type=F32), 0.0)
        acc = acc + jnp.dot((a * a).astype(BF16), wdn_ref[cols, :], preferred_element_type=F32)
    if final_norm:
        acc = _rmsnorm(acc, rest[0][...])
    o_ref[...] = acc


def _mlp(x, g, wup, wdn, g_final=None):
    n, d = x.shape
    ff = wup.shape[1]
    tile = pl.BlockSpec((ROW_TILE, d), lambda i: (i, 0))
    in_specs = [tile, _resident((1, d)), _resident((d, ff)), _resident((ff, d))]
    args = [x, g, wup, wdn]
    if g_final is not None:
        in_specs.append(_resident((1, d)))
        args.append(g_final)
    return pl.pallas_call(
        functools.partial(_mlp_kernel, final_norm=g_final is not None),
        grid=(n // ROW_TILE,),
        in_specs=in_specs,
        out_specs=tile,
        out_shape=jax.ShapeDtypeStruct((n, d), F32),
        compiler_params=_params(),
        name="mlp_final" if g_final is not None else "mlp",
    )(*args)


def _qkv_kernel(x_ref, g_ref, w_ref, o_ref):
    d = x_ref.shape[1]
    h = _rmsnorm(x_ref[...], g_ref[...]).astype(BF16)
    for t in range(3):
        u = jnp.dot(h, w_ref[:, t * d:(t + 1) * d], preferred_element_type=F32)
        if t == 0:
            u = u * (HEAD_DIM ** -0.5)
        u = u.astype(BF16)
        for p in range(N_PAIRS):
            o_ref[t, p] = u[:, p * LANES:(p + 1) * LANES]


def _qkv(x, g, w):
    n, d = x.shape
    return pl.pallas_call(
        _qkv_kernel,
        grid=(n // ROW_TILE,),
        in_specs=[pl.BlockSpec((ROW_TILE, d), lambda i: (i, 0)), _resident((1, d)), _resident((d, 3 * d))],
        out_specs=pl.BlockSpec((3, N_PAIRS, ROW_TILE, LANES), lambda i: (0, 0, i, 0)),
        out_shape=jax.ShapeDtypeStruct((3, N_PAIRS, n, LANES), BF16),
        compiler_params=_params(),
        name="qkv",
    )(x, g, w)


def _bias_table_kernel(rpb_ref, o_ref):
    shape = (GRID_W, LANES)
    c = lax.broadcasted_iota(jnp.int32, shape, 0)
    lane = lax.broadcasted_iota(jnp.int32, shape, 1)
    kc = lane & (GRID_W - 1)
    col_start = jnp.clip(c - WIN_W // 2, 0, GRID_W - WIN_W)
    in_window = (kc >= col_start) & (kc < col_start + WIN_W)
    left = lane < GRID_W
    tiles = []
    for r in range(2 * WIN_H - 1):
        row = jnp.broadcast_to(rpb_ref[0, r:r + 1, :], shape)
        t_left = pltpu.roll(row, LANES - (WIN_W - 1), axis=1, stride=1, stride_axis=0)
        t_right = pltpu.roll(row, GRID_W - (WIN_W - 1), axis=1, stride=1, stride_axis=0)
        tiles.append(jnp.where(in_window, jnp.where(left, t_left, t_right), MASKED))
    for e in range(N_FULL):
        o_ref[0, e] = jnp.where(left, tiles[e], tiles[e + 1])
    o_ref[0, E_LOW_MASKED] = jnp.where(left, MASKED, tiles[WIN_H - 1 - WIN_H // 2])
    o_ref[0, E_HIGH_MASKED] = jnp.where(left, tiles[WIN_H - 1 + WIN_H // 2 - 1], MASKED)


def _bias_table(rpb):
    h, nr, nc = rpb.shape
    padded = jnp.pad(rpb, ((0, 0), (0, 16 - nr), (0, LANES - nc)))
    return pl.pallas_call(
        _bias_table_kernel,
        grid=(h,),
        in_specs=[pl.BlockSpec((1, 16, LANES), lambda i: (i, 0, 0))],
        out_specs=pl.BlockSpec((1, N_ENTRIES, GRID_W, LANES), lambda i: (i, 0, 0, 0)),
        out_shape=jax.ShapeDtypeStruct((h, N_ENTRIES, GRID_W, LANES), F32),
        compiler_params=_params(),
        name="bias_table",
    )(padded)


def _window_tiles(case, rq):
    offset = (0, WIN_H // 2, WIN_H)[case]
    first = (0, rq, WIN_H // 2)[case]
    out = []
    for p in range(N_KEY_BLOCKS * KEY_BLOCK_ROWS // 2):
        lo_ok = first <= 2 * p < first + WIN_H
        hi_ok = first <= 2 * p + 1 < first + WIN_H
        dr = 2 * p - rq - offset
        if lo_ok and hi_ok:
            out.append((p, dr + WIN_H - 1))
        elif hi_ok:
            assert dr == -WIN_H // 2 - 1
            out.append((p, E_LOW_MASKED))
        elif lo_ok:
            assert dr == WIN_H // 2 - 1
            out.append((p, E_HIGH_MASKED))
    return out


def _attn_kernel(q_ref, k0_ref, k1_ref, k2_ref, v0_ref, v1_ref, v2_ref, tbl_ref, x_ref, wo_ref, o_ref, obuf):
    mq = q_ref.shape[1]
    step = pl.program_id(1)
    last = pl.num_programs(1) - 1
    n_key_tiles = N_KEY_BLOCKS * KEY_BLOCK_ROWS // 2
    lane = lax.broadcasted_iota(jnp.int32, (mq, LANES), 1)
    left = lane < HEAD_DIM

    def pair_body(case, p, carry):
        q = q_ref[p]
        q2 = jnp.concatenate([jnp.where(left, q, 0), jnp.where(left, 0, q)], axis=0)
        k = jnp.concatenate([k0_ref[p], k1_ref[p], k2_ref[p]], axis=0)
        v = jnp.concatenate([v0_ref[p], v1_ref[p], v2_ref[p]], axis=0)
        s = lax.dot_general(q2, k, (((1,), (1,)), ((), ())), preferred_element_type=F32)
        v_left = lax.broadcasted_iota(jnp.int32, v.shape, 1) < HEAD_DIM
        v_heads = (jnp.where(v_left, v, 1), jnp.where(v_left, 1, v))
        outs = []
        for hh in range(2):
            head = 2 * p + hh
            prob_rows = []
            for rq in range(Q_ROWS):
                r0 = hh * mq + rq * GRID_W
                tiles = _window_tiles(case, rq)
                st = [s[r0:r0 + GRID_W, t * LANES:(t + 1) * LANES] + tbl_ref[head, e] for t, e in tiles]
                m = jnp.max(functools.reduce(jnp.maximum, st), axis=1, keepdims=True)
                by_tile = {t: jnp.exp(x - m).astype(BF16) for (t, _), x in zip(tiles, st)}
                prob_rows.append(jnp.concatenate(
                    [by_tile[t] if t in by_tile else jnp.zeros((GRID_W, LANES), BF16)
                     for t in range(n_key_tiles)], axis=1))
            outs.append(jnp.dot(jnp.concatenate(prob_rows, axis=0), v_heads[hh], preferred_element_type=F32))
        numer = jnp.where(left, outs[0], outs[1])
        denom = pltpu.roll(jnp.where(left, outs[1], outs[0]), HEAD_DIM, axis=1)
        obuf[p] = (numer / denom).astype(BF16)
        return carry

    for case, cond in ((0, step == 0), (1, (step > 0) & (step < last)), (2, step == last)):
        @pl.when(cond)
        def _(case=case):
            lax.fori_loop(0, N_PAIRS, functools.partial(pair_body, case), 0, unroll=PAIR_UNROLL)

    a = jnp.concatenate([obuf[p] for p in range(N_PAIRS)], axis=1)
    o_ref[...] = x_ref[...] + jnp.dot(a, wo_ref[...], preferred_element_type=F32)


def _attn(qkv, tbl, x, wo, batch, seq):
    n, d = x.shape
    mq = Q_ROWS * GRID_W
    kb = KEY_BLOCK_ROWS * GRID_W
    steps = seq // mq
    kb_per_seq = seq // kb
    last_start = kb_per_seq - N_KEY_BLOCKS

    def kv_spec(t, j):
        def index(b, i):
            return (t, 0, b * kb_per_seq + jnp.clip(i - 1, 0, last_start) + j, 0)
        return pl.BlockSpec((None, N_PAIRS, kb, LANES), index)

    tile = pl.BlockSpec((mq, d), lambda b, i: (b * steps + i, 0))
    in_specs = [pl.BlockSpec((None, N_PAIRS, mq, LANES), lambda b, i: (0, 0, b * steps + i, 0))]
    in_specs += [kv_spec(1, j) for j in range(N_KEY_BLOCKS)]
    in_specs += [kv_spec(2, j) for j in range(N_KEY_BLOCKS)]
    in_specs += [_resident(tbl.shape), tile, _resident((d, d))]
    return pl.pallas_call(
        _attn_kernel,
        grid=(batch, steps),
        in_specs=in_specs,
        out_specs=tile,
        out_shape=jax.ShapeDtypeStruct((n, d), F32),
        scratch_shapes=[pltpu.VMEM((N_PAIRS, mq, LANES), BF16)],
        compiler_params=pltpu.CompilerParams(
            dimension_semantics=("arbitrary", "arbitrary"), vmem_limit_bytes=VMEM_LIMIT),
        name="natten",
    )(*([qkv] * (1 + 2 * N_KEY_BLOCKS)), tbl, x, wo)


def kernel(x, norm_mix, conv_w_in, conv_w, conv_w_out, attn_w_qkv, attn_rpb, attn_w_o, norm_mlp, mlp_w_up,
           mlp_w_down, norm_final):
    batch, seq, d = x.shape
    n = batch * seq
    assert d == N_HEADS * HEAD_DIM and seq % (GRID_W * Q_ROWS) == 0 and seq // GRID_W >= WIN_H + Q_ROWS
    assert seq % ROW_TILE == 0 and N_KEY_BLOCKS == 3 and Q_ROWS == KEY_BLOCK_ROWS == WIN_H // 2
    assert norm_mix.shape[0] == 2 and conv_w_in.shape[0] == 1 and attn_w_qkv.shape[0] == 1

    xf = x.reshape(n, d)
    gb, z = _conv_in(xf, norm_mix[0:1], conv_w_in[0].astype(BF16))
    xf = _conv_out(z, gb, conv_w[0], xf, conv_w_out[0].astype(BF16), seq)
    xf = _mlp(xf, norm_mlp[0:1], mlp_w_up[0].astype(BF16), mlp_w_down[0].astype(BF16))

    qkv = _qkv(xf, norm_mix[1:2], attn_w_qkv[0].astype(BF16))
    tbl = _bias_table(attn_rpb[0])
    xf = _attn(qkv, tbl, xf, attn_w_o[0].astype(BF16), batch, seq)
    out = _mlp(xf, norm_mlp[1:2], mlp_w_up[1].astype(BF16), mlp_w_down[1].astype(BF16), norm_final.reshape(1, d))
    return out.reshape(batch, seq, d)
```

```python
import functools

import jax
import jax.numpy as jnp
from jax import lax
from jax.experimental import pallas as pl
from jax.experimental.pallas import tpu as pltpu

F32 = jnp.float32
BF16 = jnp.bfloat16

NORM_EPS = 1e-6
N_HEADS = 16
HEAD_DIM = 64
GRID_W = 64
WIN_H = 8
WIN_W = 16
LANES = 128
N_PAIRS = N_HEADS * HEAD_DIM // LANES

ROW_TILE = 512
FF_CHUNK = 1024
HALO = 16
CONV_CHUNK = 512
Q_ROWS = 4
KEY_BLOCK_ROWS = 4
N_KEY_BLOCKS = Q_ROWS // KEY_BLOCK_ROWS + WIN_H // KEY_BLOCK_ROWS
MASKED = -1e30
PAIR_UNROLL = 8

N_FULL = 2 * WIN_H - 2
E_LOW_MASKED = N_FULL
E_HIGH_MASKED = N_FULL + 1
N_ENTRIES = N_FULL + 2

VMEM_LIMIT = 52 * 1024 * 1024


def _params():
    return pltpu.CompilerParams(dimension_semantics=("arbitrary",), vmem_limit_bytes=VMEM_LIMIT)


def _resident(shape):
    return pl.BlockSpec(shape, lambda *_: (0,) * len(shape), pipeline_mode=pl.Buffered(1))


def _rmsnorm(x, g):
    ms = jnp.mean(x * x, axis=-1, keepdims=True)
    return x * lax.rsqrt(ms + NORM_EPS) * g


def _conv_kernel(x_ref, xprev_ref, xnext_ref, g_ref, win_ref, cw_ref, wout_ref, o_ref, hbuf, ybuf, win_bf, wout_bf,
                 *, tiles_per_seq):
    tm, d = x_ref.shape
    rows = tm + 2 * HALO
    step = pl.program_id(0)

    @pl.when(step == 0)
    def _():
        win_bf[...] = win_ref[...].astype(BF16)
        wout_bf[...] = wout_ref[...].astype(BF16)

    pos = lax.rem(step, tiles_per_seq)
    g = g_ref[...]
    hbuf[0:HALO] = _rmsnorm(xprev_ref[...], g).astype(BF16)
    hbuf[HALO:HALO + tm] = _rmsnorm(x_ref[...], g).astype(BF16)
    hbuf[HALO + tm:rows] = _rmsnorm(xnext_ref[...], g).astype(BF16)
    h = hbuf[...]
    row = lax.broadcasted_iota(jnp.int32, (rows, 1), 0)
    outside = ((row < HALO) & (pos == 0)) | ((row >= HALO + tm) & (pos == tiles_per_seq - 1))
    for j in range(d // CONV_CHUNK):
        cols = slice(j * CONV_CHUNK, (j + 1) * CONV_CHUNK)
        c = jnp.dot(h, win_bf[:, d + j * CONV_CHUNK:d + (j + 1) * CONV_CHUNK], preferred_element_type=F32)
        v = jnp.dot(h, win_bf[:, 2 * d + j * CONV_CHUNK:2 * d + (j + 1) * CONV_CHUNK], preferred_element_type=F32)
        z = jnp.where(outside, 0.0, c * v)
        z_m1 = pltpu.roll(z, 1, axis=0)[HALO:HALO + tm]
        z_p1 = pltpu.roll(z, rows - 1, axis=0)[HALO:HALO + tm]
        zc = cw_ref[0:1, cols] * z_m1 + cw_ref[1:2, cols] * z[HALO:HALO + tm] + cw_ref[2:3, cols] * z_p1
        gate = jnp.dot(hbuf[HALO:HALO + tm], win_bf[:, cols], preferred_element_type=F32)
        ybuf[:, cols] = (gate * zc).astype(BF16)
    o_ref[...] = x_ref[...] + jnp.dot(ybuf[...], wout_bf[...], preferred_element_type=F32)


def _conv_mixer(x, g, w_in, cw, w_out, seq):
    n, d = x.shape
    per = ROW_TILE // HALO
    tile = pl.BlockSpec((ROW_TILE, d), lambda i: (i, 0))
    prev_spec = pl.BlockSpec((HALO, d), lambda i: (jnp.maximum(i * per - 1, 0), 0))
    next_spec = pl.BlockSpec((HALO, d), lambda i: (jnp.minimum((i + 1) * per, n // HALO - 1), 0))
    return pl.pallas_call(
        functools.partial(_conv_kernel, tiles_per_seq=seq // ROW_TILE),
        grid=(n // ROW_TILE,),
        in_specs=[tile, prev_spec, next_spec, _resident((1, d)), _resident((d, 3 * d)), _resident(cw.shape),
                  _resident((d, d))],
        out_specs=tile,
        out_shape=jax.ShapeDtypeStruct((n, d), F32),
        scratch_shapes=[pltpu.VMEM((ROW_TILE + 2 * HALO, d), BF16), pltpu.VMEM((ROW_TILE, d), BF16),
                        pltpu.VMEM((d, 3 * d), BF16), pltpu.VMEM((d, d), BF16)],
        compiler_params=_params(),
        name="conv_mixer",
    )(x, x, x, g, w_in, cw, w_out)


def _mlp_kernel(x_ref, g_ref, wup_ref, wdn_ref, *rest, final_norm):
    o_ref = rest[-1]
    ff = wup_ref.shape[1]
    x = x_ref[...]
    h = _rmsnorm(x, g_ref[...]).astype(BF16)
    acc = x
    for c in range(ff // FF_CHUNK):
        cols = slice(c * FF_CHUNK, (c + 1) * FF_CHUNK)
        a = jnp.maximum(jnp.dot(h, wup_ref[:, cols], preferred_element_type=F32), 0.0)
        acc = acc + jnp.dot((a * a).astype(BF16), wdn_ref[cols, :], preferred_element_type=F32)
    if final_norm:
        acc = _rmsnorm(acc, rest[0][...])
    o_ref[...] = acc


def _mlp(x, g, wup, wdn, g_final=None):
    n, d = x.shape
    ff = wup.shape[1]
    tile = pl.BlockSpec((ROW_TILE, d), lambda i: (i, 0))
    in_specs = [tile, _resident((1, d)), _resident((d, ff)), _resident((ff, d))]
    args = [x, g, wup, wdn]
    if g_final is not None:
        in_specs.append(_resident((1, d)))
        args.append(g_final)
    return pl.pallas_call(
        functools.partial(_mlp_kernel, final_norm=g_final is not None),
        grid=(n // ROW_TILE,),
        in_specs=in_specs,
        out_specs=tile,
        out_shape=jax.ShapeDtypeStruct((n, d), F32),
        compiler_params=_params(),
        name="mlp_final" if g_final is not None else "mlp",
    )(*args)


def _qkv_kernel(x_ref, g_ref, w_ref, o_ref, w_bf):
    d = x_ref.shape[1]

    @pl.when(pl.program_id(0) == 0)
    def _():
        w_bf[...] = w_ref[...].astype(BF16)

    h = _rmsnorm(x_ref[...], g_ref[...]).astype(BF16)
    for t in range(3):
        u = jnp.dot(h, w_bf[:, t * d:(t + 1) * d], preferred_element_type=F32)
        if t == 0:
            u = u * (HEAD_DIM ** -0.5)
        u = u.astype(BF16)
        for p in range(N_PAIRS):
            o_ref[t, p] = u[:, p * LANES:(p + 1) * LANES]


def _qkv(x, g, w):
    n, d = x.shape
    return pl.pallas_call(
        _qkv_kernel,
        grid=(n // ROW_TILE,),
        in_specs=[pl.BlockSpec((ROW_TILE, d), lambda i: (i, 0)), _resident((1, d)), _resident((d, 3 * d))],
        out_specs=pl.BlockSpec((3, N_PAIRS, ROW_TILE, LANES), lambda i: (0, 0, i, 0)),
        out_shape=jax.ShapeDtypeStruct((3, N_PAIRS, n, LANES), BF16),
        scratch_shapes=[pltpu.VMEM((d, 3 * d), BF16)],
        compiler_params=_params(),
        name="qkv",
    )(x, g, w)


def _bias_table_kernel(rpb_ref, o_ref):
    shape = (GRID_W, LANES)
    c = lax.broadcasted_iota(jnp.int32, shape, 0)
    lane = lax.broadcasted_iota(jnp.int32, shape, 1)
    kc = lane & (GRID_W - 1)
    col_start = jnp.clip(c - WIN_W // 2, 0, GRID_W - WIN_W)
    in_window = (kc >= col_start) & (kc < col_start + WIN_W)
    left = lane < GRID_W
    tiles = []
    for r in range(2 * WIN_H - 1):
        row = jnp.broadcast_to(rpb_ref[0, r:r + 1, :], shape)
        t_left = pltpu.roll(row, LANES - (WIN_W - 1), axis=1, stride=1, stride_axis=0)
        t_right = pltpu.roll(row, GRID_W - (WIN_W - 1), axis=1, stride=1, stride_axis=0)
        tiles.append(jnp.where(in_window, jnp.where(left, t_left, t_right), MASKED))
    for e in range(N_FULL):
        o_ref[0, e] = jnp.where(left, tiles[e], tiles[e + 1])
    o_ref[0, E_LOW_MASKED] = jnp.where(left, MASKED, tiles[WIN_H - 1 - WIN_H // 2])
    o_ref[0, E_HIGH_MASKED] = jnp.where(left, tiles[WIN_H - 1 + WIN_H // 2 - 1], MASKED)


def _bias_table(rpb):
    h, nr, nc = rpb.shape
    padded = jnp.pad(rpb, ((0, 0), (0, 16 - nr), (0, LANES - nc)))
    return pl.pallas_call(
        _bias_table_kernel,
        grid=(h,),
        in_specs=[pl.BlockSpec((1, 16, LANES), lambda i: (i, 0, 0))],
        out_specs=pl.BlockSpec((1, N_ENTRIES, GRID_W, LANES), lambda i: (i, 0, 0, 0)),
        out_shape=jax.ShapeDtypeStruct((h, N_ENTRIES, GRID_W, LANES), F32),
        compiler_params=_params(),
        name="bias_table",
    )(padded)


def _window_tiles(case, rq):
    offset = (0, WIN_H // 2, WIN_H)[case]
    first = (0, rq, WIN_H // 2)[case]
    out = []
    for p in range(N_KEY_BLOCKS * KEY_BLOCK_ROWS // 2):
        lo_ok = first <= 2 * p < first + WIN_H
        hi_ok = first <= 2 * p + 1 < first + WIN_H
        dr = 2 * p - rq - offset
        if lo_ok and hi_ok:
            out.append((p, dr + WIN_H - 1))
        elif hi_ok:
            assert dr == -WIN_H // 2 - 1
            out.append((p, E_LOW_MASKED))
        elif lo_ok:
            assert dr == WIN_H // 2 - 1
            out.append((p, E_HIGH_MASKED))
    return out


def _attn_kernel(q_ref, k0_ref, k1_ref, k2_ref, v0_ref, v1_ref, v2_ref, tbl_ref, x_ref, wo_ref, o_ref, obuf,
                 wo_bf):
    mq = q_ref.shape[1]
    step = pl.program_id(1)

    @pl.when((pl.program_id(0) == 0) & (step == 0))
    def _():
        wo_bf[...] = wo_ref[...].astype(BF16)

    last = pl.num_programs(1) - 1
    n_key_tiles = N_KEY_BLOCKS * KEY_BLOCK_ROWS // 2
    lane = lax.broadcasted_iota(jnp.int32, (mq, LANES), 1)
    left = lane < HEAD_DIM

    def pair_body(case, p, carry):
        q = q_ref[p]
        q2 = jnp.concatenate([jnp.where(left, q, 0), jnp.where(left, 0, q)], axis=0)
        k = jnp.concatenate([k0_ref[p], k1_ref[p], k2_ref[p]], axis=0)
        v = jnp.concatenate([v0_ref[p], v1_ref[p], v2_ref[p]], axis=0)
        s = lax.dot_general(q2, k, (((1,), (1,)), ((), ())), preferred_element_type=F32)
        v_left = lax.broadcasted_iota(jnp.int32, v.shape, 1) < HEAD_DIM
        v_heads = (jnp.where(v_left, v, 1), jnp.where(v_left, 1, v))
        outs = []
        for hh in range(2):
            head = 2 * p + hh
            prob_rows = []
            for rq in range(Q_ROWS):
                r0 = hh * mq + rq * GRID_W
                tiles = _window_tiles(case, rq)
                st = [s[r0:r0 + GRID_W, t * LANES:(t + 1) * LANES] + tbl_ref[head, e] for t, e in tiles]
                m = jnp.max(functools.reduce(jnp.maximum, st), axis=1, keepdims=True)
                by_tile = {t: jnp.exp(x - m).astype(BF16) for (t, _), x in zip(tiles, st)}
                prob_rows.append(jnp.concatenate(
                    [by_tile[t] if t in by_tile else jnp.zeros((GRID_W, LANES), BF16)
                     for t in range(n_key_tiles)], axis=1))
            outs.append(jnp.dot(jnp.concatenate(prob_rows, axis=0), v_heads[hh], preferred_element_type=F32))
        numer = jnp.where(left, outs[0], outs[1])
        denom = pltpu.roll(jnp.where(left, outs[1], outs[0]), HEAD_DIM, axis=1)
        obuf[p] = (numer / denom).astype(BF16)
        return carry

    for case, cond in ((0, step == 0), (1, (step > 0) & (step < last)), (2, step == last)):
        @pl.when(cond)
        def _(case=case):
            lax.fori_loop(0, N_PAIRS, functools.partial(pair_body, case), 0, unroll=PAIR_UNROLL)

    a = jnp.concatenate([obuf[p] for p in range(N_PAIRS)], axis=1)
    o_ref[...] = x_ref[...] + jnp.dot(a, wo_bf[...], preferred_element_type=F32)


def _attn(qkv, tbl, x, wo, batch, seq):
    n, d = x.shape
    mq = Q_ROWS * GRID_W
    kb = KEY_BLOCK_ROWS * GRID_W
    steps = seq // mq
    kb_per_seq = seq // kb
    last_start = kb_per_seq - N_KEY_BLOCKS

    def kv_spec(t, j):
        def index(b, i):
            return (t, 0, b * kb_per_seq + jnp.clip(i - 1, 0, last_start) + j, 0)
        return pl.BlockSpec((None, N_PAIRS, kb, LANES), index)

    tile = pl.BlockSpec((mq, d), lambda b, i: (b * steps + i, 0))
    in_specs = [pl.BlockSpec((None, N_PAIRS, mq, LANES), lambda b, i: (0, 0, b * steps + i, 0))]
    in_specs += [kv_spec(1, j) for j in range(N_KEY_BLOCKS)]
    in_specs += [kv_spec(2, j) for j in range(N_KEY_BLOCKS)]
    in_specs += [_resident(tbl.shape), tile, _resident((d, d))]
    return pl.pallas_call(
        _attn_kernel,
        grid=(batch, steps),
        in_specs=in_specs,
        out_specs=tile,
        out_shape=jax.ShapeDtypeStruct((n, d), F32),
        scratch_shapes=[pltpu.VMEM((N_PAIRS, mq, LANES), BF16), pltpu.VMEM((d, d), BF16)],
        compiler_params=pltpu.CompilerParams(
            dimension_semantics=("arbitrary", "arbitrary"), vmem_limit_bytes=VMEM_LIMIT),
        name="natten",
    )(*([qkv] * (1 + 2 * N_KEY_BLOCKS)), tbl, x, wo)


def kernel(x, norm_mix, conv_w_in, conv_w, conv_w_out, attn_w_qkv, attn_rpb, attn_w_o, norm_mlp, mlp_w_up,
           mlp_w_down, norm_final):
    batch, seq, d = x.shape
    n = batch * seq
    assert d == N_HEADS * HEAD_DIM and seq % (GRID_W * Q_ROWS) == 0 and seq // GRID_W >= WIN_H + Q_ROWS
    assert seq % ROW_TILE == 0 and ROW_TILE % HALO == 0 and d % CONV_CHUNK == 0
    assert N_KEY_BLOCKS == 3 and Q_ROWS == KEY_BLOCK_ROWS == WIN_H // 2
    assert norm_mix.shape[0] == 2 and conv_w_in.shape[0] == 1 and attn_w_qkv.shape[0] == 1

    xf = x.reshape(n, d)
    xf = _conv_mixer(xf, norm_mix[0:1], conv_w_in[0], conv_w[0], conv_w_out[0], seq)
    xf = _mlp(xf, norm_mlp[0:1], mlp_w_up[0].astype(BF16), mlp_w_down[0].astype(BF16))

    qkv = _qkv(xf, norm_mix[1:2], attn_w_qkv[0])
    tbl = _bias_table(attn_rpb[0])
    xf = _attn(qkv, tbl, xf, attn_w_o[0], batch, seq)
    out = _mlp(xf, norm_mlp[1:2], mlp_w_up[1].astype(BF16), mlp_w_down[1].astype(BF16), norm_final.reshape(1, d))
    return out.reshape(batch, seq, d)
```

```python
import functools

import jax
import jax.numpy as jnp
from jax import lax
from jax.experimental import pallas as pl
from jax.experimental.pallas import tpu as pltpu

F32 = jnp.float32
BF16 = jnp.bfloat16

NORM_EPS = 1e-6
N_HEADS = 16
HEAD_DIM = 64
GRID_W = 64
WIN_H = 8
WIN_W = 16
LANES = 128
N_PAIRS = N_HEADS * HEAD_DIM // LANES

ROW_TILE = 512
FF_CHUNK = 1024
CAST_STEPS = 8
HALO = 16
CONV_CHUNK = 512
Q_ROWS = 4
KEY_BLOCK_ROWS = 4
N_KEY_BLOCKS = Q_ROWS // KEY_BLOCK_ROWS + WIN_H // KEY_BLOCK_ROWS
MASKED = -1e30
PAIR_UNROLL = 8

N_FULL = 2 * WIN_H - 2
E_LOW_MASKED = N_FULL
E_HIGH_MASKED = N_FULL + 1
N_ENTRIES = N_FULL + 2

VMEM_LIMIT = 52 * 1024 * 1024


def _params():
    return pltpu.CompilerParams(dimension_semantics=("arbitrary",), vmem_limit_bytes=VMEM_LIMIT)


def _resident(shape):
    return pl.BlockSpec(shape, lambda *_: (0,) * len(shape), pipeline_mode=pl.Buffered(1))


def _rmsnorm(x, g):
    ms = jnp.mean(x * x, axis=-1, keepdims=True)
    return x * lax.rsqrt(ms + NORM_EPS) * g


def _conv_kernel(x_ref, xprev_ref, xnext_ref, g_ref, win_ref, cw_ref, wout_ref, o_ref, hbuf, ybuf, win_bf, wout_bf,
                 *, tiles_per_seq):
    tm, d = x_ref.shape
    rows = tm + 2 * HALO
    step = pl.program_id(0)

    @pl.when(step == 0)
    def _():
        win_bf[...] = win_ref[...].astype(BF16)
        wout_bf[...] = wout_ref[...].astype(BF16)

    pos = lax.rem(step, tiles_per_seq)
    g = g_ref[...]
    hbuf[0:HALO] = _rmsnorm(xprev_ref[...], g).astype(BF16)
    hbuf[HALO:HALO + tm] = _rmsnorm(x_ref[...], g).astype(BF16)
    hbuf[HALO + tm:rows] = _rmsnorm(xnext_ref[...], g).astype(BF16)
    h = hbuf[...]
    row = lax.broadcasted_iota(jnp.int32, (rows, 1), 0)
    outside = ((row < HALO) & (pos == 0)) | ((row >= HALO + tm) & (pos == tiles_per_seq - 1))
    for j in range(d // CONV_CHUNK):
        cols = slice(j * CONV_CHUNK, (j + 1) * CONV_CHUNK)
        c = jnp.dot(h, win_bf[:, d + j * CONV_CHUNK:d + (j + 1) * CONV_CHUNK], preferred_element_type=F32)
        v = jnp.dot(h, win_bf[:, 2 * d + j * CONV_CHUNK:2 * d + (j + 1) * CONV_CHUNK], preferred_element_type=F32)
        z = jnp.where(outside, 0.0, c * v)
        z_m1 = pltpu.roll(z, 1, axis=0)[HALO:HALO + tm]
        z_p1 = pltpu.roll(z, rows - 1, axis=0)[HALO:HALO + tm]
        zc = cw_ref[0:1, cols] * z_m1 + cw_ref[1:2, cols] * z[HALO:HALO + tm] + cw_ref[2:3, cols] * z_p1
        gate = jnp.dot(hbuf[HALO:HALO + tm], win_bf[:, cols], preferred_element_type=F32)
        ybuf[:, cols] = (gate * zc).astype(BF16)
    o_ref[...] = x_ref[...] + jnp.dot(ybuf[...], wout_bf[...], preferred_element_type=F32)


def _conv_mixer(x, g, w_in, cw, w_out, seq):
    n, d = x.shape
    per = ROW_TILE // HALO
    tile = pl.BlockSpec((ROW_TILE, d), lambda i: (i, 0))
    prev_spec = pl.BlockSpec((HALO, d), lambda i: (jnp.maximum(i * per - 1, 0), 0))
    next_spec = pl.BlockSpec((HALO, d), lambda i: (jnp.minimum((i + 1) * per, n // HALO - 1), 0))
    return pl.pallas_call(
        functools.partial(_conv_kernel, tiles_per_seq=seq // ROW_TILE),
        grid=(n // ROW_TILE,),
        in_specs=[tile, prev_spec, next_spec, _resident((1, d)), _resident((d, 3 * d)), _resident(cw.shape),
                  _resident((d, d))],
        out_specs=tile,
        out_shape=jax.ShapeDtypeStruct((n, d), F32),
        scratch_shapes=[pltpu.VMEM((ROW_TILE + 2 * HALO, d), BF16), pltpu.VMEM((ROW_TILE, d), BF16),
                        pltpu.VMEM((d, 3 * d), BF16), pltpu.VMEM((d, d), BF16)],
        compiler_params=_params(),
        name="conv_mixer",
    )(x, x, x, g, w_in, cw, w_out)


def _mlp_kernel(x_ref, g_ref, wup_ref, wdn_ref, *rest, final_norm):
    o_ref, wup_bf, wdn_bf = rest[-3:]
    step = pl.program_id(0)
    up_rows, ff = wup_ref.shape
    dn_rows = wdn_ref.shape[0]

    @pl.when(step < CAST_STEPS)
    def _():
        wup_bf[pl.ds(pl.multiple_of(step * up_rows, up_rows), up_rows), :] = wup_ref[...].astype(BF16)
        wdn_bf[pl.ds(pl.multiple_of(step * dn_rows, dn_rows), dn_rows), :] = wdn_ref[...].astype(BF16)

    @pl.when(step >= CAST_STEPS)
    def _():
        x = x_ref[...]
        h = _rmsnorm(x, g_ref[...]).astype(BF16)
        acc = x
        for c in range(ff // FF_CHUNK):
            cols = slice(c * FF_CHUNK, (c + 1) * FF_CHUNK)
            a = jnp.maximum(jnp.dot(h, wup_bf[:, cols], preferred_element_type=F32), 0.0)
            acc = acc + jnp.dot((a * a).astype(BF16), wdn_bf[cols, :], preferred_element_type=F32)
        if final_norm:
            acc = _rmsnorm(acc, rest[0][...])
        o_ref[...] = acc


def _mlp(x, g, wup, wdn, layer, g_final=None):
    n, d = x.shape
    ff = wup.shape[2]
    tile = pl.BlockSpec((ROW_TILE, d), lambda s: (jnp.maximum(s - CAST_STEPS, 0), 0))

    def chunk(s):
        return (layer, jnp.minimum(s, CAST_STEPS - 1), 0)

    in_specs = [tile, _resident((1, d)), pl.BlockSpec((None, d // CAST_STEPS, ff), chunk),
                pl.BlockSpec((None, ff // CAST_STEPS, d), chunk)]
    args = [x, g, wup, wdn]
    if g_final is not None:
        in_specs.append(_resident((1, d)))
        args.append(g_final)
    return pl.pallas_call(
        functools.partial(_mlp_kernel, final_norm=g_final is not None),
        grid=(CAST_STEPS + n // ROW_TILE,),
        in_specs=in_specs,
        out_specs=tile,
        out_shape=jax.ShapeDtypeStruct((n, d), F32),
        scratch_shapes=[pltpu.VMEM((d, ff), BF16), pltpu.VMEM((ff, d), BF16)],
        compiler_params=_params(),
        name="mlp_final" if g_final is not None else "mlp",
    )(*args)


def _qkv_kernel(x_ref, g_ref, w_ref, o_ref, w_bf):
    d = x_ref.shape[1]

    @pl.when(pl.program_id(0) == 0)
    def _():
        w_bf[...] = w_ref[...].astype(BF16)

    h = _rmsnorm(x_ref[...], g_ref[...]).astype(BF16)
    for t in range(3):
        u = jnp.dot(h, w_bf[:, t * d:(t + 1) * d], preferred_element_type=F32)
        if t == 0:
            u = u * (HEAD_DIM ** -0.5)
        u = u.astype(BF16)
        for p in range(N_PAIRS):
            o_ref[t, p] = u[:, p * LANES:(p + 1) * LANES]


def _qkv(x, g, w):
    n, d = x.shape
    return pl.pallas_call(
        _qkv_kernel,
        grid=(n // ROW_TILE,),
        in_specs=[pl.BlockSpec((ROW_TILE, d), lambda i: (i, 0)), _resident((1, d)), _resident((d, 3 * d))],
        out_specs=pl.BlockSpec((3, N_PAIRS, ROW_TILE, LANES), lambda i: (0, 0, i, 0)),
        out_shape=jax.ShapeDtypeStruct((3, N_PAIRS, n, LANES), BF16),
        scratch_shapes=[pltpu.VMEM((d, 3 * d), BF16)],
        compiler_params=_params(),
        name="qkv",
    )(x, g, w)


def _bias_table_kernel(rpb_ref, o_ref):
    shape = (GRID_W, LANES)
    c = lax.broadcasted_iota(jnp.int32, shape, 0)
    lane = lax.broadcasted_iota(jnp.int32, shape, 1)
    kc = lane & (GRID_W - 1)
    col_start = jnp.clip(c - WIN_W // 2, 0, GRID_W - WIN_W)
    in_window = (kc >= col_start) & (kc < col_start + WIN_W)
    left = lane < GRID_W
    tiles = []
    for r in range(2 * WIN_H - 1):
        row = jnp.broadcast_to(rpb_ref[0, r:r + 1, :], shape)
        t_left = pltpu.roll(row, LANES - (WIN_W - 1), axis=1, stride=1, stride_axis=0)
        t_right = pltpu.roll(row, GRID_W - (WIN_W - 1), axis=1, stride=1, stride_axis=0)
        tiles.append(jnp.where(in_window, jnp.where(left, t_left, t_right), MASKED))
    for e in range(N_FULL):
        o_ref[0, e] = jnp.where(left, tiles[e], tiles[e + 1])
    o_ref[0, E_LOW_MASKED] = jnp.where(left, MASKED, tiles[WIN_H - 1 - WIN_H // 2])
    o_ref[0, E_HIGH_MASKED] = jnp.where(left, tiles[WIN_H - 1 + WIN_H // 2 - 1], MASKED)


def _bias_table(rpb):
    h, nr, nc = rpb.shape
    padded = jnp.pad(rpb, ((0, 0), (0, 16 - nr), (0, LANES - nc)))
    return pl.pallas_call(
        _bias_table_kernel,
        grid=(h,),
        in_specs=[pl.BlockSpec((1, 16, LANES), lambda i: (i, 0, 0))],
        out_specs=pl.BlockSpec((1, N_ENTRIES, GRID_W, LANES), lambda i: (i, 0, 0, 0)),
        out_shape=jax.ShapeDtypeStruct((h, N_ENTRIES, GRID_W, LANES), F32),
        compiler_params=_params(),
        name="bias_table",
    )(padded)


def _window_tiles(case, rq):
    offset = (0, WIN_H // 2, WIN_H)[case]
    first = (0, rq, WIN_H // 2)[case]
    out = []
    for p in range(N_KEY_BLOCKS * KEY_BLOCK_ROWS // 2):
        lo_ok = first <= 2 * p < first + WIN_H
        hi_ok = first <= 2 * p + 1 < first + WIN_H
        dr = 2 * p - rq - offset
        if lo_ok and hi_ok:
            out.append((p, dr + WIN_H - 1))
        elif hi_ok:
            assert dr == -WIN_H // 2 - 1
            out.append((p, E_LOW_MASKED))
        elif lo_ok:
            assert dr == WIN_H // 2 - 1
            out.append((p, E_HIGH_MASKED))
    return out


def _attn_kernel(q_ref, k0_ref, k1_ref, k2_ref, v0_ref, v1_ref, v2_ref, tbl_ref, x_ref, wo_ref, o_ref, obuf,
                 wo_bf, *, steps_per_seq, n_blocks):
    mq = q_ref.shape[1]
    t = pl.program_id(0)
    step = lax.rem(t, steps_per_seq)
    slot = lax.rem(t, 2)

    @pl.when(t == 0)
    def _():
        wo_bf[...] = wo_ref[...].astype(BF16)
        obuf[1] = jnp.zeros(obuf.shape[1:], BF16)

    last = steps_per_seq - 1
    n_key_tiles = N_KEY_BLOCKS * KEY_BLOCK_ROWS // 2
    lane = lax.broadcasted_iota(jnp.int32, (mq, LANES), 1)
    left = lane < HEAD_DIM

    def project_previous():
        a = jnp.concatenate([obuf[1 - slot, p] for p in range(N_PAIRS)], axis=1)
        o_ref[...] = x_ref[...] + jnp.dot(a, wo_bf[...], preferred_element_type=F32)

    def pair_body(case, p, carry):
        q = q_ref[p]
        q2 = jnp.concatenate([jnp.where(left, q, 0), jnp.where(left, 0, q)], axis=0)
        k = jnp.concatenate([k0_ref[p], k1_ref[p], k2_ref[p]], axis=0)
        v = jnp.concatenate([v0_ref[p], v1_ref[p], v2_ref[p]], axis=0)
        s = lax.dot_general(q2, k, (((1,), (1,)), ((), ())), preferred_element_type=F32)
        v_left = lax.broadcasted_iota(jnp.int32, v.shape, 1) < HEAD_DIM
        v_heads = (jnp.where(v_left, v, 1), jnp.where(v_left, 1, v))
        outs = []
        for hh in range(2):
            head = 2 * p + hh
            prob_rows = []
            for rq in range(Q_ROWS):
                r0 = hh * mq + rq * GRID_W
                tiles = _window_tiles(case, rq)
                st = [s[r0:r0 + GRID_W, t * LANES:(t + 1) * LANES] + tbl_ref[head, e] for t, e in tiles]
                m = jnp.max(functools.reduce(jnp.maximum, st), axis=1, keepdims=True)
                by_tile = {t: jnp.exp(x - m).astype(BF16) for (t, _), x in zip(tiles, st)}
                prob_rows.append(jnp.concatenate(
                    [by_tile[t] if t in by_tile else jnp.zeros((GRID_W, LANES), BF16)
                     for t in range(n_key_tiles)], axis=1))
            outs.append(jnp.dot(jnp.concatenate(prob_rows, axis=0), v_heads[hh], preferred_element_type=F32))
        numer = jnp.where(left, outs[0], outs[1])
        denom = pltpu.roll(jnp.where(left, outs[1], outs[0]), HEAD_DIM, axis=1)
        obuf[slot, p] = (numer / denom).astype(BF16)
        return carry

    for case, cond in ((0, step == 0), (1, (step > 0) & (step < last)), (2, step == last)):
        @pl.when(cond & (t < n_blocks))
        def _(case=case):
            project_previous()
            lax.fori_loop(0, N_PAIRS, functools.partial(pair_body, case), 0, unroll=PAIR_UNROLL)

    pl.when(t == n_blocks)(project_previous)


def _attn(qkv, tbl, x, wo, seq):
    n, d = x.shape
    mq = Q_ROWS * GRID_W
    kb = KEY_BLOCK_ROWS * GRID_W
    steps = seq // mq
    n_blocks = n // mq
    last_start = seq // kb - N_KEY_BLOCKS

    def q_index(t):
        return (0, 0, jnp.minimum(t, n_blocks - 1), 0)

    def kv_spec(which, j):
        def index(t):
            blk = jnp.minimum(t, n_blocks - 1)
            first = (blk // steps) * (seq // kb) + jnp.clip(blk % steps - 1, 0, last_start)
            return (which, 0, first + j, 0)
        return pl.BlockSpec((None, N_PAIRS, kb, LANES), index)

    tile = pl.BlockSpec((mq, d), lambda t: (jnp.maximum(t - 1, 0), 0))
    in_specs = [pl.BlockSpec((None, N_PAIRS, mq, LANES), q_index)]
    in_specs += [kv_spec(1, j) for j in range(N_KEY_BLOCKS)]
    in_specs += [kv_spec(2, j) for j in range(N_KEY_BLOCKS)]
    in_specs += [_resident(tbl.shape), tile, _resident((d, d))]
    return pl.pallas_call(
        functools.partial(_attn_kernel, steps_per_seq=steps, n_blocks=n_blocks),
        grid=(n_blocks + 1,),
        in_specs=in_specs,
        out_specs=tile,
        out_shape=jax.ShapeDtypeStruct((n, d), F32),
        scratch_shapes=[pltpu.VMEM((2, N_PAIRS, mq, LANES), BF16), pltpu.VMEM((d, d), BF16)],
        compiler_params=_params(),
        name="natten",
    )(*([qkv] * (1 + 2 * N_KEY_BLOCKS)), tbl, x, wo)


def kernel(x, norm_mix, conv_w_in, conv_w, conv_w_out, attn_w_qkv, attn_rpb, attn_w_o, norm_mlp, mlp_w_up,
           mlp_w_down, norm_final):
    batch, seq, d = x.shape
    n = batch * seq
    assert d == N_HEADS * HEAD_DIM and seq % (GRID_W * Q_ROWS) == 0 and seq // GRID_W >= WIN_H + Q_ROWS
    assert seq % ROW_TILE == 0 and ROW_TILE % HALO == 0 and d % CONV_CHUNK == 0 and d % (8 * CAST_STEPS) == 0
    assert N_KEY_BLOCKS == 3 and Q_ROWS == KEY_BLOCK_ROWS == WIN_H // 2
    assert norm_mix.shape[0] == 2 and conv_w_in.shape[0] == 1 and attn_w_qkv.shape[0] == 1

    xf = x.reshape(n, d)
    xf = _conv_mixer(xf, norm_mix[0:1], conv_w_in[0], conv_w[0], conv_w_out[0], seq)
    xf = _mlp(xf, norm_mlp[0:1], mlp_w_up, mlp_w_down, 0)

    qkv = _qkv(xf, norm_mix[1:2], attn_w_qkv[0])
    tbl = _bias_table(attn_rpb[0])
    xf = _attn(qkv, tbl, xf, attn_w_o[0], seq)
    out = _mlp(xf, norm_mlp[1:2], mlp_w_up, mlp_w_down, 1, norm_final.reshape(1, d))
    return out.reshape(batch, seq, d)
```

```python
import functools

import jax
import jax.numpy as jnp
from jax import lax
from jax.experimental import pallas as pl
from jax.experimental.pallas import tpu as pltpu

F32 = jnp.float32
BF16 = jnp.bfloat16

NORM_EPS = 1e-6
N_HEADS = 16
HEAD_DIM = 64
GRID_W = 64
WIN_H = 8
WIN_W = 16
LANES = 128
N_PAIRS = N_HEADS * HEAD_DIM // LANES

ROW_TILE = 512
FF_CHUNK = 1024
CAST_STEPS = 8
HALO = 16
CONV_CHUNK = 256
Q_ROWS = 4
KEY_BLOCK_ROWS = 4
KEY_BLOCK = KEY_BLOCK_ROWS * GRID_W
N_KEY_BLOCKS = Q_ROWS // KEY_BLOCK_ROWS + WIN_H // KEY_BLOCK_ROWS
MASKED = -1e30
PROJ_EVERY = 2
PROJ_CHUNK = N_HEADS * HEAD_DIM * PROJ_EVERY // N_PAIRS

N_FULL = 2 * WIN_H - 2
E_RIGHT_MASKED = N_FULL
E_LEFT_MASKED = N_FULL + 1
N_ENTRIES = N_FULL + 2
ONES_ROWS = 16

VMEM_LIMIT = 52 * 1024 * 1024


def _params():
    return pltpu.CompilerParams(dimension_semantics=("arbitrary",), vmem_limit_bytes=VMEM_LIMIT)


def _resident(shape):
    return pl.BlockSpec(shape, lambda *_: (0,) * len(shape), pipeline_mode=pl.Buffered(1))


def _rmsnorm(x, g):
    ms = jnp.mean(x * x, axis=-1, keepdims=True)
    return x * lax.rsqrt(ms + NORM_EPS) * g


def _conv_kernel(x_ref, xprev_ref, xnext_ref, g_ref, win_ref, cw_ref, wout_ref, o_ref, hbuf, ybuf, win_bf, wout_bf,
                 *, tiles_per_seq):
    tm, d = x_ref.shape
    rows = tm + 2 * HALO
    step = pl.program_id(0)

    @pl.when(step == 0)
    def _():
        win_bf[...] = win_ref[...].astype(BF16)
        wout_bf[...] = wout_ref[...].astype(BF16)

    pos = lax.rem(step, tiles_per_seq)
    g = g_ref[...]
    hbuf[0:HALO] = _rmsnorm(xprev_ref[...], g).astype(BF16)
    hbuf[HALO:HALO + tm] = _rmsnorm(x_ref[...], g).astype(BF16)
    hbuf[HALO + tm:rows] = _rmsnorm(xnext_ref[...], g).astype(BF16)
    h = hbuf[...]
    row = lax.broadcasted_iota(jnp.int32, (rows, 1), 0)
    outside = ((row < HALO) & (pos == 0)) | ((row >= HALO + tm) & (pos == tiles_per_seq - 1))
    for j in range(d // CONV_CHUNK):
        cols = slice(j * CONV_CHUNK, (j + 1) * CONV_CHUNK)
        c = jnp.dot(h, win_bf[:, d + j * CONV_CHUNK:d + (j + 1) * CONV_CHUNK], preferred_element_type=F32)
        v = jnp.dot(h, win_bf[:, 2 * d + j * CONV_CHUNK:2 * d + (j + 1) * CONV_CHUNK], preferred_element_type=F32)
        z = jnp.where(outside, 0.0, c * v)
        z_m1 = pltpu.roll(z, 1, axis=0)[HALO:HALO + tm]
        z_p1 = pltpu.roll(z, rows - 1, axis=0)[HALO:HALO + tm]
        zc = cw_ref[0:1, cols] * z_m1 + cw_ref[1:2, cols] * z[HALO:HALO + tm] + cw_ref[2:3, cols] * z_p1
        gate = jnp.dot(hbuf[HALO:HALO + tm], win_bf[:, cols], preferred_element_type=F32)
        ybuf[:, cols] = (gate * zc).astype(BF16)
    o_ref[...] = x_ref[...] + jnp.dot(ybuf[...], wout_bf[...], preferred_element_type=F32)


def _conv_mixer(x, g, w_in, cw, w_out, seq):
    n, d = x.shape
    per = ROW_TILE // HALO
    tile = pl.BlockSpec((ROW_TILE, d), lambda i: (i, 0))
    prev_spec = pl.BlockSpec((HALO, d), lambda i: (jnp.maximum(i * per - 1, 0), 0))
    next_spec = pl.BlockSpec((HALO, d), lambda i: (jnp.minimum((i + 1) * per, n // HALO - 1), 0))
    return pl.pallas_call(
        functools.partial(_conv_kernel, tiles_per_seq=seq // ROW_TILE),
        grid=(n // ROW_TILE,),
        in_specs=[tile, prev_spec, next_spec, _resident((1, d)), _resident((d, 3 * d)), _resident(cw.shape),
                  _resident((d, d))],
        out_specs=tile,
        out_shape=jax.ShapeDtypeStruct((n, d), F32),
        scratch_shapes=[pltpu.VMEM((ROW_TILE + 2 * HALO, d), BF16), pltpu.VMEM((ROW_TILE, d), BF16),
                        pltpu.VMEM((d, 3 * d), BF16), pltpu.VMEM((d, d), BF16)],
        compiler_params=_params(),
        name="conv_mixer",
    )(x, x, x, g, w_in, cw, w_out)


def _mlp_kernel(x_ref, g_ref, wup_ref, wdn_ref, *rest, final_norm):
    o_ref, wup_bf, wdn_bf = rest[-3:]
    step = pl.program_id(0)
    up_rows, ff = wup_ref.shape
    dn_rows = wdn_ref.shape[0]

    @pl.when(step < CAST_STEPS)
    def _():
        wup_bf[pl.ds(pl.multiple_of(step * up_rows, up_rows), up_rows), :] = wup_ref[...].astype(BF16)
        wdn_bf[pl.ds(pl.multiple_of(step * dn_rows, dn_rows), dn_rows), :] = wdn_ref[...].astype(BF16)

    @pl.when(step >= CAST_STEPS)
    def _():
        x = x_ref[...]
        h = _rmsnorm(x, g_ref[...]).astype(BF16)
        acc = x
        for c in range(ff // FF_CHUNK):
            cols = slice(c * FF_CHUNK, (c + 1) * FF_CHUNK)
            a = jnp.maximum(jnp.dot(h, wup_bf[:, cols], preferred_element_type=F32), 0.0)
            acc = acc + jnp.dot((a * a).astype(BF16), wdn_bf[cols, :], preferred_element_type=F32)
        if final_norm:
            acc = _rmsnorm(acc, rest[0][...])
        o_ref[...] = acc


def _mlp(x, g, wup, wdn, layer, g_final=None):
    n, d = x.shape
    ff = wup.shape[2]
    tile = pl.BlockSpec((ROW_TILE, d), lambda s: (jnp.maximum(s - CAST_STEPS, 0), 0))

    def chunk(s):
        return (layer, jnp.minimum(s, CAST_STEPS - 1), 0)

    in_specs = [tile, _resident((1, d)), pl.BlockSpec((None, d // CAST_STEPS, ff), chunk),
                pl.BlockSpec((None, ff // CAST_STEPS, d), chunk)]
    args = [x, g, wup, wdn]
    if g_final is not None:
        in_specs.append(_resident((1, d)))
        args.append(g_final)
    return pl.pallas_call(
        functools.partial(_mlp_kernel, final_norm=g_final is not None),
        grid=(CAST_STEPS + n // ROW_TILE,),
        in_specs=in_specs,
        out_specs=tile,
        out_shape=jax.ShapeDtypeStruct((n, d), F32),
        scratch_shapes=[pltpu.VMEM((d, ff), BF16), pltpu.VMEM((ff, d), BF16)],
        compiler_params=_params(),
        name="mlp_final" if g_final is not None else "mlp",
    )(*args)


def _qkv_kernel(x_ref, g_ref, w_ref, qk_ref, vt_ref, w_bf):
    d = x_ref.shape[1]

    @pl.when(pl.program_id(0) == 0)
    def _():
        w_bf[...] = w_ref[...].astype(BF16)

    h = _rmsnorm(x_ref[...], g_ref[...]).astype(BF16)
    for t in range(3):
        u = jnp.dot(h, w_bf[:, t * d:(t + 1) * d], preferred_element_type=F32)
        if t == 0:
            u = u * (HEAD_DIM ** -0.5)
        for p in range(N_PAIRS):
            blk = u[:, p * LANES:(p + 1) * LANES]
            if t < 2:
                qk_ref[t, p] = blk.astype(BF16)
            else:
                for b in range(vt_ref.shape[0]):
                    vt_ref[b, p] = blk[b * KEY_BLOCK:(b + 1) * KEY_BLOCK].T.astype(BF16)


def _qkv(x, g, w):
    n, d = x.shape
    return pl.pallas_call(
        _qkv_kernel,
        grid=(n // ROW_TILE,),
        in_specs=[pl.BlockSpec((ROW_TILE, d), lambda i: (i, 0)), _resident((1, d)), _resident((d, 3 * d))],
        out_specs=[pl.BlockSpec((2, N_PAIRS, ROW_TILE, LANES), lambda i: (0, 0, i, 0)),
                   pl.BlockSpec((ROW_TILE // KEY_BLOCK, N_PAIRS, LANES, KEY_BLOCK), lambda i: (i, 0, 0, 0))],
        out_shape=[jax.ShapeDtypeStruct((2, N_PAIRS, n, LANES), BF16),
                   jax.ShapeDtypeStruct((n // KEY_BLOCK, N_PAIRS, LANES, KEY_BLOCK), BF16)],
        scratch_shapes=[pltpu.VMEM((d, 3 * d), BF16)],
        compiler_params=_params(),
        name="qkv",
    )(x, g, w)


def _bias_table_kernel(rpb_ref, o_ref):
    shape = (GRID_W, LANES)
    kc = lax.broadcasted_iota(jnp.int32, shape, 0)
    lane = lax.broadcasted_iota(jnp.int32, shape, 1)
    c = lane & (GRID_W - 1)
    col_start = jnp.clip(c - WIN_W // 2, 0, GRID_W - WIN_W)
    in_window = (kc >= col_start) & (kc < col_start + WIN_W)
    left = lane < GRID_W
    tiles = []
    for r in range(2 * WIN_H - 1):
        row = jnp.broadcast_to(rpb_ref[0, r:r + 1, :], shape)
        t_left = pltpu.roll(row, LANES - (WIN_W - 1), axis=1, stride=1, stride_axis=0)
        t_right = pltpu.roll(row, GRID_W - (WIN_W - 1), axis=1, stride=1, stride_axis=0)
        tiles.append(jnp.where(in_window, jnp.where(left, t_left, t_right), MASKED))
    for e in range(N_FULL):
        o_ref[0, e] = jnp.where(left, tiles[e + 1], tiles[e])
    o_ref[0, E_RIGHT_MASKED] = jnp.where(left, tiles[WIN_H - 1 - WIN_H // 2], MASKED)
    o_ref[0, E_LEFT_MASKED] = jnp.where(left, MASKED, tiles[WIN_H - 1 + WIN_H // 2 - 1])


def _bias_table(rpb):
    h, nr, nc = rpb.shape
    padded = jnp.pad(rpb[:, :, ::-1], ((0, 0), (0, 16 - nr), (0, LANES - nc)))
    return pl.pallas_call(
        _bias_table_kernel,
        grid=(h,),
        in_specs=[pl.BlockSpec((1, 16, LANES), lambda i: (i, 0, 0))],
        out_specs=pl.BlockSpec((1, N_ENTRIES, GRID_W, LANES), lambda i: (i, 0, 0, 0)),
        out_shape=jax.ShapeDtypeStruct((h, N_ENTRIES, GRID_W, LANES), F32),
        compiler_params=_params(),
        name="bias_table",
    )(padded)


def _window_rows(case, j):
    offset = (0, WIN_H // 2, WIN_H)[case]

    def in_window(rq, rk):
        first = (0, rq, WIN_H // 2)[case]
        return first <= rk < first + WIN_H

    out = []
    for rk in range(N_KEY_BLOCKS * KEY_BLOCK_ROWS):
        left_ok, right_ok = in_window(2 * j, rk), in_window(2 * j + 1, rk)
        dr_left = rk - 2 * j - offset
        if left_ok and right_ok:
            out.append((rk, dr_left + WIN_H - 2))
        elif left_ok:
            assert dr_left == -WIN_H // 2
            out.append((rk, E_RIGHT_MASKED))
        elif right_ok:
            assert dr_left - 1 == WIN_H // 2 - 1
            out.append((rk, E_LEFT_MASKED))
    return out


def _attn_kernel(q_ref, k0_ref, k1_ref, k2_ref, v0_ref, v1_ref, v2_ref, tbl_ref, x_ref, wo_ref, o_ref, obuf,
                 wo_bf, *, steps_per_seq, n_blocks):
    mq = q_ref.shape[1]
    t = pl.program_id(0)
    step = lax.rem(t, steps_per_seq)
    slot = lax.rem(t, 2)

    @pl.when(t == 0)
    def _():
        wo_bf[...] = wo_ref[...].astype(BF16)
        obuf[1] = jnp.zeros(obuf.shape[1:], BF16)

    last = steps_per_seq - 1
    n_key_rows = N_KEY_BLOCKS * KEY_BLOCK_ROWS
    left = lax.broadcasted_iota(jnp.int32, (mq, LANES), 1) < HEAD_DIM
    ones_rows = jnp.ones((ONES_ROWS, n_key_rows * GRID_W), BF16)

    def project_previous(part=None):
        a = jnp.concatenate([obuf[1 - slot, p] for p in range(N_PAIRS)], axis=1)
        cols = slice(None) if part is None else slice(part * PROJ_CHUNK, (part + 1) * PROJ_CHUNK)
        o_ref[:, cols] = x_ref[:, cols] + jnp.dot(a, wo_bf[:, cols], preferred_element_type=F32)

    def scores(p):
        q = q_ref[p]
        q2 = jnp.concatenate([jnp.where(left, q, 0), jnp.where(left, 0, q)], axis=0)
        k = jnp.concatenate([k0_ref[p], k1_ref[p], k2_ref[p]], axis=0)
        return lax.dot_general(k, q2, (((1,), (1,)), ((), ())), preferred_element_type=F32)

    def finish(case, p, st):
        vt = jnp.concatenate([v0_ref[p], v1_ref[p], v2_ref[p]], axis=1)
        outs = []
        for hh in range(2):
            head = 2 * p + hh
            cols = []
            for j in range(Q_ROWS // 2):
                lt = 2 * hh + j
                rows = _window_rows(case, j)
                sc = [st[rk * GRID_W:(rk + 1) * GRID_W, lt * LANES:(lt + 1) * LANES] + tbl_ref[head, e]
                      for rk, e in rows]
                m = jnp.max(functools.reduce(jnp.maximum, sc), axis=0, keepdims=True)
                by_row = {rk: jnp.exp(x - m).astype(BF16) for (rk, _), x in zip(rows, sc)}
                cols.append(jnp.concatenate(
                    [by_row[rk] if rk in by_row else jnp.zeros((GRID_W, LANES), BF16)
                     for rk in range(n_key_rows)], axis=0))
            probs_t = jnp.concatenate(cols, axis=1)
            v_ones = jnp.concatenate([vt[hh * HEAD_DIM:(hh + 1) * HEAD_DIM], ones_rows], axis=0)
            o_t = jnp.dot(v_ones, probs_t, preferred_element_type=F32)
            outs.append(o_t[:HEAD_DIM] / o_t[HEAD_DIM:HEAD_DIM + 1])
        obuf[slot, p] = jnp.concatenate(outs, axis=0).T.astype(BF16)

    for case, cond in ((0, step == 0), (1, (step > 0) & (step < last)), (2, step == last)):
        @pl.when(cond & (t < n_blocks))
        def _(case=case):
            st = scores(0)
            for p in range(N_PAIRS):
                st_next = scores(p + 1) if p + 1 < N_PAIRS else None
                if p % PROJ_EVERY == PROJ_EVERY - 1:
                    project_previous(p // PROJ_EVERY)
                finish(case, p, st)
                st = st_next

    pl.when(t == n_blocks)(project_previous)


def _attn(qk, vt, tbl, x, wo, seq):
    n, d = x.shape
    mq = Q_ROWS * GRID_W
    kb = KEY_BLOCK
    steps = seq // mq
    n_blocks = n // mq
    last_start = seq // kb - N_KEY_BLOCKS

    def first_key_block(t):
        blk = jnp.minimum(t, n_blocks - 1)
        return (blk // steps) * (seq // kb) + jnp.clip(blk % steps - 1, 0, last_start)

    tile = pl.BlockSpec((mq, d), lambda t: (jnp.maximum(t - 1, 0), 0))
    in_specs = [pl.BlockSpec((None, N_PAIRS, mq, LANES), lambda t: (0, 0, jnp.minimum(t, n_blocks - 1), 0))]
    in_specs += [pl.BlockSpec((None, N_PAIRS, kb, LANES), lambda t, j=j: (1, 0, first_key_block(t) + j, 0))
                 for j in range(N_KEY_BLOCKS)]
    in_specs += [pl.BlockSpec((None, N_PAIRS, LANES, kb), lambda t, j=j: (first_key_block(t) + j, 0, 0, 0))
                 for j in range(N_KEY_BLOCKS)]
    in_specs += [_resident(tbl.shape), tile, _resident((d, d))]
    return pl.pallas_call(
        functools.partial(_attn_kernel, steps_per_seq=steps, n_blocks=n_blocks),
        grid=(n_blocks + 1,),
        in_specs=in_specs,
        out_specs=tile,
        out_shape=jax.ShapeDtypeStruct((n, d), F32),
        scratch_shapes=[pltpu.VMEM((2, N_PAIRS, mq, LANES), BF16), pltpu.VMEM((d, d), BF16)],
        compiler_params=_params(),
        name="natten",
    )(*([qk] * (1 + N_KEY_BLOCKS)), *([vt] * N_KEY_BLOCKS), tbl, x, wo)


def kernel(x, norm_mix, conv_w_in, conv_w, conv_w_out, attn_w_qkv, attn_rpb, attn_w_o, norm_mlp, mlp_w_up,
           mlp_w_down, norm_final):
    batch, seq, d = x.shape
    n = batch * seq
    assert d == N_HEADS * HEAD_DIM and seq % (GRID_W * Q_ROWS) == 0 and seq // GRID_W >= WIN_H + Q_ROWS
    assert seq % ROW_TILE == 0 and ROW_TILE % HALO == 0 and d % CONV_CHUNK == 0 and d % (8 * CAST_STEPS) == 0
    assert N_KEY_BLOCKS == 3 and Q_ROWS == KEY_BLOCK_ROWS == WIN_H // 2 and ROW_TILE % KEY_BLOCK == 0
    assert norm_mix.shape[0] == 2 and conv_w_in.shape[0] == 1 and attn_w_qkv.shape[0] == 1

    xf = x.reshape(n, d)
    xf = _conv_mixer(xf, norm_mix[0:1], conv_w_in[0], conv_w[0], conv_w_out[0], seq)
    xf = _mlp(xf, norm_mlp[0:1], mlp_w_up, mlp_w_down, 0)

    qk, vt = _qkv(xf, norm_mix[1:2], attn_w_qkv[0])
    tbl = _bias_table(attn_rpb[0])
    xf = _attn(qk, vt, tbl, xf, attn_w_o[0], seq)
    out = _mlp(xf, norm_mlp[1:2], mlp_w_up, mlp_w_down, 1, norm_final.reshape(1, d))
    return out.reshape(batch, seq, d)
```

```python
import functools

import jax
import jax.numpy as jnp
from jax import lax
from jax.experimental import pallas as pl
from jax.experimental.pallas import tpu as pltpu

F32 = jnp.float32
BF16 = jnp.bfloat16

NORM_EPS = 1e-6
N_HEADS = 16
HEAD_DIM = 64
GRID_W = 64
WIN_H = 8
WIN_W = 16
LANES = 128
N_PAIRS = N_HEADS * HEAD_DIM // LANES

ROW_TILE = 512
MLP_TILE = 1024
QKV_TILE = 1024
FF_CHUNK = 512
CAST_STEPS = 16
HALO = 16
CONV_CHUNK = 256
Q_ROWS = 4
KEY_BLOCK_ROWS = 4
KEY_BLOCK = KEY_BLOCK_ROWS * GRID_W
N_KEY_BLOCKS = Q_ROWS // KEY_BLOCK_ROWS + WIN_H // KEY_BLOCK_ROWS
MASKED = -1e30
PROJ_EVERY = 2
PROJ_CHUNK = N_HEADS * HEAD_DIM * PROJ_EVERY // N_PAIRS

N_FULL = 2 * WIN_H - 2
E_RIGHT_MASKED = N_FULL
E_LEFT_MASKED = N_FULL + 1
N_ENTRIES = N_FULL + 2
ONES_ROWS = 16

VMEM_LIMIT = 52 * 1024 * 1024


def _params():
    return pltpu.CompilerParams(dimension_semantics=("arbitrary",), vmem_limit_bytes=VMEM_LIMIT)


def _resident(shape):
    return pl.BlockSpec(shape, lambda *_: (0,) * len(shape), pipeline_mode=pl.Buffered(1))


def _rmsnorm(x, g):
    ms = jnp.mean(x * x, axis=-1, keepdims=True)
    return x * lax.rsqrt(ms + NORM_EPS) * g


def _conv_kernel(x_ref, xprev_ref, xnext_ref, g_ref, win_ref, cw_ref, wout_ref, o_ref, hbuf, ybuf, win_bf, wout_bf,
                 *, tiles_per_seq):
    tm, d = x_ref.shape
    rows = tm + 2 * HALO
    step = pl.program_id(0)

    @pl.when(step == 0)
    def _():
        win_bf[...] = win_ref[...].astype(BF16)
        wout_bf[...] = wout_ref[...].astype(BF16)

    pos = lax.rem(step, tiles_per_seq)
    g = g_ref[...]
    hbuf[0:HALO] = _rmsnorm(xprev_ref[...], g).astype(BF16)
    hbuf[HALO:HALO + tm] = _rmsnorm(x_ref[...], g).astype(BF16)
    hbuf[HALO + tm:rows] = _rmsnorm(xnext_ref[...], g).astype(BF16)
    h = hbuf[...]
    row = lax.broadcasted_iota(jnp.int32, (rows, 1), 0)
    outside = ((row < HALO) & (pos == 0)) | ((row >= HALO + tm) & (pos == tiles_per_seq - 1))
    for j in range(d // CONV_CHUNK):
        cols = slice(j * CONV_CHUNK, (j + 1) * CONV_CHUNK)
        c = jnp.dot(h, win_bf[:, d + j * CONV_CHUNK:d + (j + 1) * CONV_CHUNK], preferred_element_type=F32)
        v = jnp.dot(h, win_bf[:, 2 * d + j * CONV_CHUNK:2 * d + (j + 1) * CONV_CHUNK], preferred_element_type=F32)
        z = jnp.where(outside, 0.0, c * v)
        z_m1 = pltpu.roll(z, 1, axis=0)[HALO:HALO + tm]
        z_p1 = pltpu.roll(z, rows - 1, axis=0)[HALO:HALO + tm]
        zc = cw_ref[0:1, cols] * z_m1 + cw_ref[1:2, cols] * z[HALO:HALO + tm] + cw_ref[2:3, cols] * z_p1
        gate = jnp.dot(hbuf[HALO:HALO + tm], win_bf[:, cols], preferred_element_type=F32)
        ybuf[:, cols] = (gate * zc).astype(BF16)
    o_ref[...] = x_ref[...] + jnp.dot(ybuf[...], wout_bf[...], preferred_element_type=F32)


def _conv_mixer(x, g, w_in, cw, w_out, seq):
    n, d = x.shape
    per = ROW_TILE // HALO
    tile = pl.BlockSpec((ROW_TILE, d), lambda i: (i, 0))
    prev_spec = pl.BlockSpec((HALO, d), lambda i: (jnp.maximum(i * per - 1, 0), 0))
    next_spec = pl.BlockSpec((HALO, d), lambda i: (jnp.minimum((i + 1) * per, n // HALO - 1), 0))
    return pl.pallas_call(
        functools.partial(_conv_kernel, tiles_per_seq=seq // ROW_TILE),
        grid=(n // ROW_TILE,),
        in_specs=[tile, prev_spec, next_spec, _resident((1, d)), _resident((d, 3 * d)), _resident(cw.shape),
                  _resident((d, d))],
        out_specs=tile,
        out_shape=jax.ShapeDtypeStruct((n, d), F32),
        scratch_shapes=[pltpu.VMEM((ROW_TILE + 2 * HALO, d), BF16), pltpu.VMEM((ROW_TILE, d), BF16),
                        pltpu.VMEM((d, 3 * d), BF16), pltpu.VMEM((d, d), BF16)],
        compiler_params=_params(),
        name="conv_mixer",
    )(x, x, x, g, w_in, cw, w_out)


def _mlp_kernel(x_ref, g_ref, wup_ref, wdn_ref, *rest, final_norm):
    o_ref, wup_bf, wdn_bf = rest[-3:]
    step = pl.program_id(0)
    up_rows, ff = wup_ref.shape
    dn_rows = wdn_ref.shape[0]

    @pl.when(step < CAST_STEPS)
    def _():
        wup_bf[pl.ds(pl.multiple_of(step * up_rows, up_rows), up_rows), :] = wup_ref[...].astype(BF16)
        wdn_bf[pl.ds(pl.multiple_of(step * dn_rows, dn_rows), dn_rows), :] = wdn_ref[...].astype(BF16)

    @pl.when(step >= CAST_STEPS)
    def _():
        x = x_ref[...]
        h = _rmsnorm(x, g_ref[...]).astype(BF16)
        acc = x
        for c in range(ff // FF_CHUNK):
            cols = slice(c * FF_CHUNK, (c + 1) * FF_CHUNK)
            a = jnp.maximum(jnp.dot(h, wup_bf[:, cols], preferred_element_type=F32), 0.0)
            acc = acc + jnp.dot((a * a).astype(BF16), wdn_bf[cols, :], preferred_element_type=F32)
        if final_norm:
            acc = _rmsnorm(acc, rest[0][...])
        o_ref[...] = acc


def _mlp(x, g, wup, wdn, layer, g_final=None):
    n, d = x.shape
    ff = wup.shape[2]
    tile = pl.BlockSpec((MLP_TILE, d), lambda s: (jnp.maximum(s - CAST_STEPS, 0), 0))

    def chunk(s):
        return (layer, jnp.minimum(s, CAST_STEPS - 1), 0)

    in_specs = [tile, _resident((1, d)), pl.BlockSpec((None, d // CAST_STEPS, ff), chunk),
                pl.BlockSpec((None, ff // CAST_STEPS, d), chunk)]
    args = [x, g, wup, wdn]
    if g_final is not None:
        in_specs.append(_resident((1, d)))
        args.append(g_final)
    return pl.pallas_call(
        functools.partial(_mlp_kernel, final_norm=g_final is not None),
        grid=(CAST_STEPS + n // MLP_TILE,),
        in_specs=in_specs,
        out_specs=tile,
        out_shape=jax.ShapeDtypeStruct((n, d), F32),
        scratch_shapes=[pltpu.VMEM((d, ff), BF16), pltpu.VMEM((ff, d), BF16)],
        compiler_params=_params(),
        name="mlp_final" if g_final is not None else "mlp",
    )(*args)


def _qkv_kernel(x_ref, g_ref, w_ref, qk_ref, vt_ref, w_bf):
    d = x_ref.shape[1]

    @pl.when(pl.program_id(0) == 0)
    def _():
        w_bf[...] = w_ref[...].astype(BF16)

    h = _rmsnorm(x_ref[...], g_ref[...]).astype(BF16)
    for t in range(3):
        u = jnp.dot(h, w_bf[:, t * d:(t + 1) * d], preferred_element_type=F32)
        if t == 0:
            u = u * (HEAD_DIM ** -0.5)
        for p in range(N_PAIRS):
            blk = u[:, p * LANES:(p + 1) * LANES]
            if t < 2:
                qk_ref[t, p] = blk.astype(BF16)
            else:
                for b in range(vt_ref.shape[0]):
                    vt_ref[b, p] = blk[b * KEY_BLOCK:(b + 1) * KEY_BLOCK].T.astype(BF16)


def _qkv(x, g, w):
    n, d = x.shape
    return pl.pallas_call(
        _qkv_kernel,
        grid=(n // QKV_TILE,),
        in_specs=[pl.BlockSpec((QKV_TILE, d), lambda i: (i, 0)), _resident((1, d)), _resident((d, 3 * d))],
        out_specs=[pl.BlockSpec((2, N_PAIRS, QKV_TILE, LANES), lambda i: (0, 0, i, 0)),
                   pl.BlockSpec((QKV_TILE // KEY_BLOCK, N_PAIRS, LANES, KEY_BLOCK), lambda i: (i, 0, 0, 0))],
        out_shape=[jax.ShapeDtypeStruct((2, N_PAIRS, n, LANES), BF16),
                   jax.ShapeDtypeStruct((n // KEY_BLOCK, N_PAIRS, LANES, KEY_BLOCK), BF16)],
        scratch_shapes=[pltpu.VMEM((d, 3 * d), BF16)],
        compiler_params=_params(),
        name="qkv",
    )(x, g, w)


def _bias_table_kernel(rpb_ref, o_ref):
    shape = (GRID_W, LANES)
    kc = lax.broadcasted_iota(jnp.int32, shape, 0)
    lane = lax.broadcasted_iota(jnp.int32, shape, 1)
    c = lane & (GRID_W - 1)
    col_start = jnp.clip(c - WIN_W // 2, 0, GRID_W - WIN_W)
    in_window = (kc >= col_start) & (kc < col_start + WIN_W)
    left = lane < GRID_W
    tiles = []
    for r in range(2 * WIN_H - 1):
        row = jnp.broadcast_to(rpb_ref[0, r:r + 1, :], shape)
        t_left = pltpu.roll(row, LANES - (WIN_W - 1), axis=1, stride=1, stride_axis=0)
        t_right = pltpu.roll(row, GRID_W - (WIN_W - 1), axis=1, stride=1, stride_axis=0)
        tiles.append(jnp.where(in_window, jnp.where(left, t_left, t_right), MASKED))
    for e in range(N_FULL):
        o_ref[0, e] = jnp.where(left, tiles[e + 1], tiles[e])
    o_ref[0, E_RIGHT_MASKED] = jnp.where(left, tiles[WIN_H - 1 - WIN_H // 2], MASKED)
    o_ref[0, E_LEFT_MASKED] = jnp.where(left, MASKED, tiles[WIN_H - 1 + WIN_H // 2 - 1])


def _bias_table(rpb):
    h, nr, nc = rpb.shape
    padded = jnp.pad(rpb[:, :, ::-1], ((0, 0), (0, 16 - nr), (0, LANES - nc)))
    return pl.pallas_call(
        _bias_table_kernel,
        grid=(h,),
        in_specs=[pl.BlockSpec((1, 16, LANES), lambda i: (i, 0, 0))],
        out_specs=pl.BlockSpec((1, N_ENTRIES, GRID_W, LANES), lambda i: (i, 0, 0, 0)),
        out_shape=jax.ShapeDtypeStruct((h, N_ENTRIES, GRID_W, LANES), F32),
        compiler_params=_params(),
        name="bias_table",
    )(padded)


def _window_rows(case, j):
    offset = (0, WIN_H // 2, WIN_H)[case]

    def in_window(rq, rk):
        first = (0, rq, WIN_H // 2)[case]
        return first <= rk < first + WIN_H

    out = []
    for rk in range(N_KEY_BLOCKS * KEY_BLOCK_ROWS):
        left_ok, right_ok = in_window(2 * j, rk), in_window(2 * j + 1, rk)
        dr_left = rk - 2 * j - offset
        if left_ok and right_ok:
            out.append((rk, dr_left + WIN_H - 2))
        elif left_ok:
            assert dr_left == -WIN_H // 2
            out.append((rk, E_RIGHT_MASKED))
        elif right_ok:
            assert dr_left - 1 == WIN_H // 2 - 1
            out.append((rk, E_LEFT_MASKED))
    return out


def _attn_kernel(q_ref, k0_ref, k1_ref, k2_ref, v0_ref, v1_ref, v2_ref, tbl_ref, x_ref, wo_ref, o_ref, obuf,
                 wo_bf, *, steps_per_seq, n_blocks):
    mq = q_ref.shape[1]
    t = pl.program_id(0)
    step = lax.rem(t, steps_per_seq)
    slot = lax.rem(t, 2)

    @pl.when(t == 0)
    def _():
        wo_bf[...] = wo_ref[...].astype(BF16)
        obuf[1] = jnp.zeros(obuf.shape[1:], BF16)

    last = steps_per_seq - 1
    n_key_rows = N_KEY_BLOCKS * KEY_BLOCK_ROWS
    left = lax.broadcasted_iota(jnp.int32, (mq, LANES), 1) < HEAD_DIM
    ones_rows = jnp.ones((ONES_ROWS, n_key_rows * GRID_W), BF16)

    def project_previous(part=None):
        a = jnp.concatenate([obuf[1 - slot, p] for p in range(N_PAIRS)], axis=1)
        cols = slice(None) if part is None else slice(part * PROJ_CHUNK, (part + 1) * PROJ_CHUNK)
        o_ref[:, cols] = x_ref[:, cols] + jnp.dot(a, wo_bf[:, cols], preferred_element_type=F32)

    def scores(p):
        q = q_ref[p]
        q2 = jnp.concatenate([jnp.where(left, q, 0), jnp.where(left, 0, q)], axis=0)
        k = jnp.concatenate([k0_ref[p], k1_ref[p], k2_ref[p]], axis=0)
        return lax.dot_general(k, q2, (((1,), (1,)), ((), ())), preferred_element_type=F32)

    def finish(case, p, st):
        vt = jnp.concatenate([v0_ref[p], v1_ref[p], v2_ref[p]], axis=1)
        outs = []
        for hh in range(2):
            head = 2 * p + hh
            cols = []
            for j in range(Q_ROWS // 2):
                lt = 2 * hh + j
                rows = _window_rows(case, j)
                sc = [st[rk * GRID_W:(rk + 1) * GRID_W, lt * LANES:(lt + 1) * LANES] + tbl_ref[head, e]
                      for rk, e in rows]
                m = jnp.max(functools.reduce(jnp.maximum, sc), axis=0, keepdims=True)
                by_row = {rk: jnp.exp(x - m).astype(BF16) for (rk, _), x in zip(rows, sc)}
                cols.append(jnp.concatenate(
                    [by_row[rk] if rk in by_row else jnp.zeros((GRID_W, LANES), BF16)
                     for rk in range(n_key_rows)], axis=0))
            probs_t = jnp.concatenate(cols, axis=1)
            v_ones = jnp.concatenate([vt[hh * HEAD_DIM:(hh + 1) * HEAD_DIM], ones_rows], axis=0)
            o_t = jnp.dot(v_ones, probs_t, preferred_element_type=F32)
            outs.append(o_t[:HEAD_DIM] / o_t[HEAD_DIM:HEAD_DIM + 1])
        obuf[slot, p] = jnp.concatenate(outs, axis=0).T.astype(BF16)

    for case, cond in ((0, step == 0), (1, (step > 0) & (step < last)), (2, step == last)):
        @pl.when(cond & (t < n_blocks))
        def _(case=case):
            st = scores(0)
            for p in range(N_PAIRS):
                st_next = scores(p + 1) if p + 1 < N_PAIRS else None
                if p % PROJ_EVERY == PROJ_EVERY - 1:
                    project_previous(p // PROJ_EVERY)
                finish(case, p, st)
                st = st_next

    pl.when(t == n_blocks)(project_previous)


def _attn(qk, vt, tbl, x, wo, seq):
    n, d = x.shape
    mq = Q_ROWS * GRID_W
    kb = KEY_BLOCK
    steps = seq // mq
    n_blocks = n // mq
    last_start = seq // kb - N_KEY_BLOCKS

    def first_key_block(t):
        blk = jnp.minimum(t, n_blocks - 1)
        return (blk // steps) * (seq // kb) + jnp.clip(blk % steps - 1, 0, last_start)

    tile = pl.BlockSpec((mq, d), lambda t: (jnp.maximum(t - 1, 0), 0))
    in_specs = [pl.BlockSpec((None, N_PAIRS, mq, LANES), lambda t: (0, 0, jnp.minimum(t, n_blocks - 1), 0))]
    in_specs += [pl.BlockSpec((None, N_PAIRS, kb, LANES), lambda t, j=j: (1, 0, first_key_block(t) + j, 0))
                 for j in range(N_KEY_BLOCKS)]
    in_specs += [pl.BlockSpec((None, N_PAIRS, LANES, kb), lambda t, j=j: (first_key_block(t) + j, 0, 0, 0))
                 for j in range(N_KEY_BLOCKS)]
    in_specs += [_resident(tbl.shape), tile, _resident((d, d))]
    return pl.pallas_call(
        functools.partial(_attn_kernel, steps_per_seq=steps, n_blocks=n_blocks),
        grid=(n_blocks + 1,),
        in_specs=in_specs,
        out_specs=tile,
        out_shape=jax.ShapeDtypeStruct((n, d), F32),
        scratch_shapes=[pltpu.VMEM((2, N_PAIRS, mq, LANES), BF16), pltpu.VMEM((d, d), BF16)],
        compiler_params=_params(),
        name="natten",
    )(*([qk] * (1 + N_KEY_BLOCKS)), *([vt] * N_KEY_BLOCKS), tbl, x, wo)


def kernel(x, norm_mix, conv_w_in, conv_w, conv_w_out, attn_w_qkv, attn_rpb, attn_w_o, norm_mlp, mlp_w_up,
           mlp_w_down, norm_final):
    batch, seq, d = x.shape
    n = batch * seq
    assert d == N_HEADS * HEAD_DIM and seq % (GRID_W * Q_ROWS) == 0 and seq // GRID_W >= WIN_H + Q_ROWS
    assert seq % ROW_TILE == 0 and ROW_TILE % HALO == 0 and d % CONV_CHUNK == 0 and d % (8 * CAST_STEPS) == 0
    assert N_KEY_BLOCKS == 3 and Q_ROWS == KEY_BLOCK_ROWS == WIN_H // 2 and QKV_TILE % KEY_BLOCK == 0
    assert n % MLP_TILE == 0 and n % QKV_TILE == 0
    assert norm_mix.shape[0] == 2 and conv_w_in.shape[0] == 1 and attn_w_qkv.shape[0] == 1

    xf = x.reshape(n, d)
    xf = _conv_mixer(xf, norm_mix[0:1], conv_w_in[0], conv_w[0], conv_w_out[0], seq)
    xf = _mlp(xf, norm_mlp[0:1], mlp_w_up, mlp_w_down, 0)

    qk, vt = _qkv(xf, norm_mix[1:2], attn_w_qkv[0])
    tbl = _bias_table(attn_rpb[0])
    xf = _attn(qk, vt, tbl, xf, attn_w_o[0], seq)
    out = _mlp(xf, norm_mlp[1:2], mlp_w_up, mlp_w_down, 1, norm_final.reshape(1, d))
    return out.reshape(batch, seq, d)
```

```python
import functools

import jax
import jax.numpy as jnp
from jax import lax
from jax.experimental import pallas as pl
from jax.experimental.pallas import tpu as pltpu

F32 = jnp.float32
BF16 = jnp.bfloat16

NORM_EPS = 1e-6
N_HEADS = 16
HEAD_DIM = 64
GRID_W = 64
WIN_H = 8
WIN_W = 16
LANES = 128
N_PAIRS = N_HEADS * HEAD_DIM // LANES

ROW_TILE = 1024
FF_CHUNK = 512
CAST_STEPS = 16
HALO = 16
CONV_CHUNK = 256
Q_ROWS = 4
KEY_BLOCK_ROWS = 4
KEY_BLOCK = KEY_BLOCK_ROWS * GRID_W
N_KEY_BLOCKS = Q_ROWS // KEY_BLOCK_ROWS + WIN_H // KEY_BLOCK_ROWS
MASKED = -1e30
PROJ_EVERY = 2
PROJ_CHUNK = N_HEADS * HEAD_DIM * PROJ_EVERY // N_PAIRS

N_FULL = 2 * WIN_H - 2
E_RIGHT_MASKED = N_FULL
E_LEFT_MASKED = N_FULL + 1
N_ENTRIES = N_FULL + 2
ONES_ROWS = 16

VMEM_LIMIT = 52 * 1024 * 1024


def _params():
    return pltpu.CompilerParams(dimension_semantics=("arbitrary",), vmem_limit_bytes=VMEM_LIMIT)


def _resident(shape):
    return pl.BlockSpec(shape, lambda *_: (0,) * len(shape), pipeline_mode=pl.Buffered(1))


def _rmsnorm(x, g):
    ms = jnp.mean(x * x, axis=-1, keepdims=True)
    return x * lax.rsqrt(ms + NORM_EPS) * g


def _conv_kernel(x_ref, xprev_ref, xnext_ref, g_ref, win_ref, cw_ref, wout_ref, o_ref, hbuf, ybuf, win_bf, wout_bf,
                 *, tiles_per_seq):
    tm, d = x_ref.shape
    rows = tm + 2 * HALO
    step = pl.program_id(0)

    @pl.when(step == 0)
    def _():
        win_bf[...] = win_ref[...].astype(BF16)
        wout_bf[...] = wout_ref[...].astype(BF16)

    pos = lax.rem(step, tiles_per_seq)
    g = g_ref[...]
    hbuf[0:HALO] = _rmsnorm(xprev_ref[...], g).astype(BF16)
    hbuf[HALO:HALO + tm] = _rmsnorm(x_ref[...], g).astype(BF16)
    hbuf[HALO + tm:rows] = _rmsnorm(xnext_ref[...], g).astype(BF16)
    h = hbuf[...]
    row = lax.broadcasted_iota(jnp.int32, (rows, 1), 0)
    outside = ((row < HALO) & (pos == 0)) | ((row >= HALO + tm) & (pos == tiles_per_seq - 1))
    for j in range(d // CONV_CHUNK):
        cols = slice(j * CONV_CHUNK, (j + 1) * CONV_CHUNK)
        c = jnp.dot(h, win_bf[:, d + j * CONV_CHUNK:d + (j + 1) * CONV_CHUNK], preferred_element_type=F32)
        v = jnp.dot(h, win_bf[:, 2 * d + j * CONV_CHUNK:2 * d + (j + 1) * CONV_CHUNK], preferred_element_type=F32)
        z = jnp.where(outside, 0.0, c * v)
        z_m1 = pltpu.roll(z, 1, axis=0)[HALO:HALO + tm]
        z_p1 = pltpu.roll(z, rows - 1, axis=0)[HALO:HALO + tm]
        zc = cw_ref[0:1, cols] * z_m1 + cw_ref[1:2, cols] * z[HALO:HALO + tm] + cw_ref[2:3, cols] * z_p1
        gate = jnp.dot(hbuf[HALO:HALO + tm], win_bf[:, cols], preferred_element_type=F32)
        ybuf[:, cols] = (gate * zc).astype(BF16)
    o_ref[...] = x_ref[...] + jnp.dot(ybuf[...], wout_bf[...], preferred_element_type=F32)


def _conv_mixer(x, g, w_in, cw, w_out, seq):
    n, d = x.shape
    per = ROW_TILE // HALO
    tile = pl.BlockSpec((ROW_TILE, d), lambda i: (i, 0))
    prev_spec = pl.BlockSpec((HALO, d), lambda i: (jnp.maximum(i * per - 1, 0), 0))
    next_spec = pl.BlockSpec((HALO, d), lambda i: (jnp.minimum((i + 1) * per, n // HALO - 1), 0))
    return pl.pallas_call(
        functools.partial(_conv_kernel, tiles_per_seq=seq // ROW_TILE),
        grid=(n // ROW_TILE,),
        in_specs=[tile, prev_spec, next_spec, _resident((1, d)), _resident((d, 3 * d)), _resident(cw.shape),
                  _resident((d, d))],
        out_specs=tile,
        out_shape=jax.ShapeDtypeStruct((n, d), F32),
        scratch_shapes=[pltpu.VMEM((ROW_TILE + 2 * HALO, d), BF16), pltpu.VMEM((ROW_TILE, d), BF16),
                        pltpu.VMEM((d, 3 * d), BF16), pltpu.VMEM((d, d), BF16)],
        compiler_params=_params(),
        name="conv_mixer",
    )(x, x, x, g, w_in, cw, w_out)


def _mlp_kernel(x_ref, g_ref, wup_ref, wdn_ref, *rest, final_norm):
    o_ref, wup_bf, wdn_bf = rest[-3:]
    step = pl.program_id(0)
    up_rows, ff = wup_ref.shape
    dn_rows = wdn_ref.shape[0]

    @pl.when(step < CAST_STEPS)
    def _():
        wup_bf[pl.ds(pl.multiple_of(step * up_rows, up_rows), up_rows), :] = wup_ref[...].astype(BF16)
        wdn_bf[pl.ds(pl.multiple_of(step * dn_rows, dn_rows), dn_rows), :] = wdn_ref[...].astype(BF16)

    @pl.when(step >= CAST_STEPS)
    def _():
        x = x_ref[...]
        h = _rmsnorm(x, g_ref[...]).astype(BF16)
        acc = x
        for c in range(ff // FF_CHUNK):
            cols = slice(c * FF_CHUNK, (c + 1) * FF_CHUNK)
            a = jnp.maximum(jnp.dot(h, wup_bf[:, cols], preferred_element_type=F32), 0.0)
            acc = acc + jnp.dot((a * a).astype(BF16), wdn_bf[cols, :], preferred_element_type=F32)
        if final_norm:
            acc = _rmsnorm(acc, rest[0][...])
        o_ref[...] = acc


def _mlp(x, g, wup, wdn, layer, g_final=None):
    n, d = x.shape
    ff = wup.shape[2]
    tile = pl.BlockSpec((ROW_TILE, d), lambda s: (jnp.maximum(s - CAST_STEPS, 0), 0))

    def chunk(s):
        return (layer, jnp.minimum(s, CAST_STEPS - 1), 0)

    in_specs = [tile, _resident((1, d)), pl.BlockSpec((None, d // CAST_STEPS, ff), chunk),
                pl.BlockSpec((None, ff // CAST_STEPS, d), chunk)]
    args = [x, g, wup, wdn]
    if g_final is not None:
        in_specs.append(_resident((1, d)))
        args.append(g_final)
    return pl.pallas_call(
        functools.partial(_mlp_kernel, final_norm=g_final is not None),
        grid=(CAST_STEPS + n // ROW_TILE,),
        in_specs=in_specs,
        out_specs=tile,
        out_shape=jax.ShapeDtypeStruct((n, d), F32),
        scratch_shapes=[pltpu.VMEM((d, ff), BF16), pltpu.VMEM((ff, d), BF16)],
        compiler_params=_params(),
        name="mlp_final" if g_final is not None else "mlp",
    )(*args)


def _qkv_kernel(x_ref, g_ref, w_ref, qk_ref, vt_ref, w_bf):
    d = x_ref.shape[1]

    @pl.when(pl.program_id(0) == 0)
    def _():
        w_bf[...] = w_ref[...].astype(BF16)

    h = _rmsnorm(x_ref[...], g_ref[...]).astype(BF16)
    for t in range(3):
        u = jnp.dot(h, w_bf[:, t * d:(t + 1) * d], preferred_element_type=F32)
        if t == 0:
            u = u * (HEAD_DIM ** -0.5)
        for p in range(N_PAIRS):
            blk = u[:, p * LANES:(p + 1) * LANES]
            if t < 2:
                qk_ref[t, p] = blk.astype(BF16)
            else:
                for b in range(vt_ref.shape[0]):
                    vt_ref[b, p] = blk[b * KEY_BLOCK:(b + 1) * KEY_BLOCK].T.astype(BF16)


def _qkv(x, g, w):
    n, d = x.shape
    return pl.pallas_call(
        _qkv_kernel,
        grid=(n // ROW_TILE,),
        in_specs=[pl.BlockSpec((ROW_TILE, d), lambda i: (i, 0)), _resident((1, d)), _resident((d, 3 * d))],
        out_specs=[pl.BlockSpec((2, N_PAIRS, ROW_TILE, LANES), lambda i: (0, 0, i, 0)),
                   pl.BlockSpec((ROW_TILE // KEY_BLOCK, N_PAIRS, LANES, KEY_BLOCK), lambda i: (i, 0, 0, 0))],
        out_shape=[jax.ShapeDtypeStruct((2, N_PAIRS, n, LANES), BF16),
                   jax.ShapeDtypeStruct((n // KEY_BLOCK, N_PAIRS, LANES, KEY_BLOCK), BF16)],
        scratch_shapes=[pltpu.VMEM((d, 3 * d), BF16)],
        compiler_params=_params(),
        name="qkv",
    )(x, g, w)


def _bias_table_kernel(rpb_ref, o_ref):
    shape = (GRID_W, LANES)
    kc = lax.broadcasted_iota(jnp.int32, shape, 0)
    lane = lax.broadcasted_iota(jnp.int32, shape, 1)
    c = lane & (GRID_W - 1)
    col_start = jnp.clip(c - WIN_W // 2, 0, GRID_W - WIN_W)
    in_window = (kc >= col_start) & (kc < col_start + WIN_W)
    left = lane < GRID_W
    tiles = []
    for r in range(2 * WIN_H - 1):
        row = jnp.broadcast_to(rpb_ref[0, r:r + 1, :], shape)
        t_left = pltpu.roll(row, LANES - (WIN_W - 1), axis=1, stride=1, stride_axis=0)
        t_right = pltpu.roll(row, GRID_W - (WIN_W - 1), axis=1, stride=1, stride_axis=0)
        tiles.append(jnp.where(in_window, jnp.where(left, t_left, t_right), MASKED))
    for e in range(N_FULL):
        o_ref[0, e] = jnp.where(left, tiles[e + 1], tiles[e])
    o_ref[0, E_RIGHT_MASKED] = jnp.where(left, tiles[WIN_H - 1 - WIN_H // 2], MASKED)
    o_ref[0, E_LEFT_MASKED] = jnp.where(left, MASKED, tiles[WIN_H - 1 + WIN_H // 2 - 1])


def _bias_table(rpb):
    h, nr, nc = rpb.shape
    padded = jnp.pad(rpb[:, :, ::-1], ((0, 0), (0, 16 - nr), (0, LANES - nc)))
    return pl.pallas_call(
        _bias_table_kernel,
        grid=(h,),
        in_specs=[pl.BlockSpec((1, 16, LANES), lambda i: (i, 0, 0))],
        out_specs=pl.BlockSpec((1, N_ENTRIES, GRID_W, LANES), lambda i: (i, 0, 0, 0)),
        out_shape=jax.ShapeDtypeStruct((h, N_ENTRIES, GRID_W, LANES), F32),
        compiler_params=_params(),
        name="bias_table",
    )(padded)


def _window_rows(case, j):
    offset = (0, WIN_H // 2, WIN_H)[case]

    def in_window(rq, rk):
        first = (0, rq, WIN_H // 2)[case]
        return first <= rk < first + WIN_H

    out = []
    for rk in range(N_KEY_BLOCKS * KEY_BLOCK_ROWS):
        left_ok, right_ok = in_window(2 * j, rk), in_window(2 * j + 1, rk)
        dr_left = rk - 2 * j - offset
        if left_ok and right_ok:
            out.append((rk, dr_left + WIN_H - 2))
        elif left_ok:
            assert dr_left == -WIN_H // 2
            out.append((rk, E_RIGHT_MASKED))
        elif right_ok:
            assert dr_left - 1 == WIN_H // 2 - 1
            out.append((rk, E_LEFT_MASKED))
    return out


def _attn_kernel(q_ref, k0_ref, k1_ref, k2_ref, v0_ref, v1_ref, v2_ref, tbl_ref, x_ref, wo_ref, o_ref, obuf,
                 wo_bf, *, steps_per_seq, n_blocks):
    mq = q_ref.shape[1]
    t = pl.program_id(0)
    step = lax.rem(t, steps_per_seq)
    slot = lax.rem(t, 2)

    @pl.when(t == 0)
    def _():
        wo_bf[...] = wo_ref[...].astype(BF16)
        obuf[1] = jnp.zeros(obuf.shape[1:], BF16)

    last = steps_per_seq - 1
    n_key_rows = N_KEY_BLOCKS * KEY_BLOCK_ROWS
    left = lax.broadcasted_iota(jnp.int32, (mq, LANES), 1) < HEAD_DIM
    ones_rows = jnp.ones((ONES_ROWS, n_key_rows * GRID_W), BF16)

    def project_previous(part=None):
        a = jnp.concatenate([obuf[1 - slot, p] for p in range(N_PAIRS)], axis=1)
        cols = slice(None) if part is None else slice(part * PROJ_CHUNK, (part + 1) * PROJ_CHUNK)
        o_ref[:, cols] = x_ref[:, cols] + jnp.dot(a, wo_bf[:, cols], preferred_element_type=F32)

    def scores(p):
        q = q_ref[p]
        q2 = jnp.concatenate([jnp.where(left, q, 0), jnp.where(left, 0, q)], axis=0)
        k = jnp.concatenate([k0_ref[p], k1_ref[p], k2_ref[p]], axis=0)
        return lax.dot_general(k, q2, (((1,), (1,)), ((), ())), preferred_element_type=F32)

    def finish(case, p, st):
        vt = jnp.concatenate([v0_ref[p], v1_ref[p], v2_ref[p]], axis=1)
        outs = []
        for hh in range(2):
            head = 2 * p + hh
            cols = []
            for j in range(Q_ROWS // 2):
                lt = 2 * hh + j
                rows = _window_rows(case, j)
                sc = [st[rk * GRID_W:(rk + 1) * GRID_W, lt * LANES:(lt + 1) * LANES] + tbl_ref[head, e]
                      for rk, e in rows]
                m = jnp.max(functools.reduce(jnp.maximum, sc), axis=0, keepdims=True)
                by_row = {rk: jnp.exp(x - m).astype(BF16) for (rk, _), x in zip(rows, sc)}
                cols.append(jnp.concatenate(
                    [by_row[rk] if rk in by_row else jnp.zeros((GRID_W, LANES), BF16)
                     for rk in range(n_key_rows)], axis=0))
            probs_t = jnp.concatenate(cols, axis=1)
            v_ones = jnp.concatenate([vt[hh * HEAD_DIM:(hh + 1) * HEAD_DIM], ones_rows], axis=0)
            o_t = jnp.dot(v_ones, probs_t, preferred_element_type=F32)
            outs.append(o_t[:HEAD_DIM] / o_t[HEAD_DIM:HEAD_DIM + 1])
        obuf[slot, p] = jnp.concatenate(outs, axis=0).T.astype(BF16)

    for case, cond in ((0, step == 0), (1, (step > 0) & (step < last)), (2, step == last)):
        @pl.when(cond & (t < n_blocks))
        def _(case=case):
            st = scores(0)
            for p in range(N_PAIRS):
                st_next = scores(p + 1) if p + 1 < N_PAIRS else None
                if p % PROJ_EVERY == PROJ_EVERY - 1:
                    project_previous(p // PROJ_EVERY)
                finish(case, p, st)
                st = st_next

    pl.when(t == n_blocks)(project_previous)


def _attn(qk, vt, tbl, x, wo, seq):
    n, d = x.shape
    mq = Q_ROWS * GRID_W
    kb = KEY_BLOCK
    steps = seq // mq
    n_blocks = n // mq
    last_start = seq // kb - N_KEY_BLOCKS

    def first_key_block(t):
        blk = jnp.minimum(t, n_blocks - 1)
        return (blk // steps) * (seq // kb) + jnp.clip(blk % steps - 1, 0, last_start)

    tile = pl.BlockSpec((mq, d), lambda t: (jnp.maximum(t - 1, 0), 0))
    in_specs = [pl.BlockSpec((None, N_PAIRS, mq, LANES), lambda t: (0, 0, jnp.minimum(t, n_blocks - 1), 0))]
    in_specs += [pl.BlockSpec((None, N_PAIRS, kb, LANES), lambda t, j=j: (1, 0, first_key_block(t) + j, 0))
                 for j in range(N_KEY_BLOCKS)]
    in_specs += [pl.BlockSpec((None, N_PAIRS, LANES, kb), lambda t, j=j: (first_key_block(t) + j, 0, 0, 0))
                 for j in range(N_KEY_BLOCKS)]
    in_specs += [_resident(tbl.shape), tile, _resident((d, d))]
    return pl.pallas_call(
        functools.partial(_attn_kernel, steps_per_seq=steps, n_blocks=n_blocks),
        grid=(n_blocks + 1,),
        in_specs=in_specs,
        out_specs=tile,
        out_shape=jax.ShapeDtypeStruct((n, d), F32),
        scratch_shapes=[pltpu.VMEM((2, N_PAIRS, mq, LANES), BF16), pltpu.VMEM((d, d), BF16)],
        compiler_params=_params(),
        name="natten",
    )(*([qk] * (1 + N_KEY_BLOCKS)), *([vt] * N_KEY_BLOCKS), tbl, x, wo)


def kernel(x, norm_mix, conv_w_in, conv_w, conv_w_out, attn_w_qkv, attn_rpb, attn_w_o, norm_mlp, mlp_w_up,
           mlp_w_down, norm_final):
    batch, seq, d = x.shape
    n = batch * seq
    assert d == N_HEADS * HEAD_DIM and seq % (GRID_W * Q_ROWS) == 0 and seq // GRID_W >= WIN_H + Q_ROWS
    assert seq % ROW_TILE == 0 and ROW_TILE % HALO == 0 and d % CONV_CHUNK == 0 and d % (8 * CAST_STEPS) == 0
    assert N_KEY_BLOCKS == 3 and Q_ROWS == KEY_BLOCK_ROWS == WIN_H // 2 and ROW_TILE % KEY_BLOCK == 0
    assert norm_mix.shape[0] == 2 and conv_w_in.shape[0] == 1 and attn_w_qkv.shape[0] == 1

    xf = x.reshape(n, d)
    xf = _conv_mixer(xf, norm_mix[0:1], conv_w_in[0], conv_w[0], conv_w_out[0], seq)
    xf = _mlp(xf, norm_mlp[0:1], mlp_w_up, mlp_w_down, 0)

    qk, vt = _qkv(xf, norm_mix[1:2], attn_w_qkv[0])
    tbl = _bias_table(attn_rpb[0])
    xf = _attn(qk, vt, tbl, xf, attn_w_o[0], seq)
    out = _mlp(xf, norm_mlp[1:2], mlp_w_up, mlp_w_down, 1, norm_final.reshape(1, d))
    return out.reshape(batch, seq, d)
```

```python
import functools

import jax
import jax.numpy as jnp
from jax import lax
from jax.experimental import pallas as pl
from jax.experimental.pallas import tpu as pltpu

F32 = jnp.float32
BF16 = jnp.bfloat16

NORM_EPS = 1e-6
N_HEADS = 16
HEAD_DIM = 64
GRID_W = 64
WIN_H = 8
WIN_W = 16
LANES = 128
N_PAIRS = N_HEADS * HEAD_DIM // LANES

ROW_TILE = 1024
FF_CHUNK = 512
CAST_STEPS = 16
HALO = 16
CONV_CHUNK = 256
Q_ROWS = 4
KEY_BLOCK_ROWS = 4
KEY_BLOCK = KEY_BLOCK_ROWS * GRID_W
N_KEY_BLOCKS = Q_ROWS // KEY_BLOCK_ROWS + WIN_H // KEY_BLOCK_ROWS
SUB_BLOCKS = 2
STEP_KEY_BLOCKS = N_KEY_BLOCKS + SUB_BLOCKS - 1
MASKED = -1e30
PROJ_CHUNK = 256

N_FULL = 2 * WIN_H - 2
E_RIGHT_MASKED = N_FULL
E_LEFT_MASKED = N_FULL + 1
N_ENTRIES = N_FULL + 2
ONES_ROWS = 16

VMEM_LIMIT = 52 * 1024 * 1024


def _params():
    return pltpu.CompilerParams(dimension_semantics=("arbitrary",), vmem_limit_bytes=VMEM_LIMIT)


def _resident(shape):
    return pl.BlockSpec(shape, lambda *_: (0,) * len(shape), pipeline_mode=pl.Buffered(1))


def _rmsnorm(x, g):
    ms = jnp.mean(x * x, axis=-1, keepdims=True)
    return x * lax.rsqrt(ms + NORM_EPS) * g


def _conv_kernel(x_ref, xprev_ref, xnext_ref, g_ref, win_ref, cw_ref, wout_ref, o_ref, hbuf, ybuf, win_bf, wout_bf,
                 *, tiles_per_seq):
    tm, d = x_ref.shape
    rows = tm + 2 * HALO
    step = pl.program_id(0)

    @pl.when(step == 0)
    def _():
        win_bf[...] = win_ref[...].astype(BF16)
        wout_bf[...] = wout_ref[...].astype(BF16)

    pos = lax.rem(step, tiles_per_seq)
    g = g_ref[...]
    hbuf[0:HALO] = _rmsnorm(xprev_ref[...], g).astype(BF16)
    hbuf[HALO:HALO + tm] = _rmsnorm(x_ref[...], g).astype(BF16)
    hbuf[HALO + tm:rows] = _rmsnorm(xnext_ref[...], g).astype(BF16)
    h = hbuf[...]
    row = lax.broadcasted_iota(jnp.int32, (rows, 1), 0)
    outside = ((row < HALO) & (pos == 0)) | ((row >= HALO + tm) & (pos == tiles_per_seq - 1))
    for j in range(d // CONV_CHUNK):
        cols = slice(j * CONV_CHUNK, (j + 1) * CONV_CHUNK)
        c = jnp.dot(h, win_bf[:, d + j * CONV_CHUNK:d + (j + 1) * CONV_CHUNK], preferred_element_type=F32)
        v = jnp.dot(h, win_bf[:, 2 * d + j * CONV_CHUNK:2 * d + (j + 1) * CONV_CHUNK], preferred_element_type=F32)
        z = jnp.where(outside, 0.0, c * v)
        z_m1 = pltpu.roll(z, 1, axis=0)[HALO:HALO + tm]
        z_p1 = pltpu.roll(z, rows - 1, axis=0)[HALO:HALO + tm]
        zc = cw_ref[0:1, cols] * z_m1 + cw_ref[1:2, cols] * z[HALO:HALO + tm] + cw_ref[2:3, cols] * z_p1
        gate = jnp.dot(hbuf[HALO:HALO + tm], win_bf[:, cols], preferred_element_type=F32)
        ybuf[:, cols] = (gate * zc).astype(BF16)
    o_ref[...] = x_ref[...] + jnp.dot(ybuf[...], wout_bf[...], preferred_element_type=F32)


def _conv_mixer(x, g, w_in, cw, w_out, seq):
    n, d = x.shape
    per = ROW_TILE // HALO
    tile = pl.BlockSpec((ROW_TILE, d), lambda i: (i, 0))
    prev_spec = pl.BlockSpec((HALO, d), lambda i: (jnp.maximum(i * per - 1, 0), 0))
    next_spec = pl.BlockSpec((HALO, d), lambda i: (jnp.minimum((i + 1) * per, n // HALO - 1), 0))
    return pl.pallas_call(
        functools.partial(_conv_kernel, tiles_per_seq=seq // ROW_TILE),
        grid=(n // ROW_TILE,),
        in_specs=[tile, prev_spec, next_spec, _resident((1, d)), _resident((d, 3 * d)), _resident(cw.shape),
                  _resident((d, d))],
        out_specs=tile,
        out_shape=jax.ShapeDtypeStruct((n, d), F32),
        scratch_shapes=[pltpu.VMEM((ROW_TILE + 2 * HALO, d), BF16), pltpu.VMEM((ROW_TILE, d), BF16),
                        pltpu.VMEM((d, 3 * d), BF16), pltpu.VMEM((d, d), BF16)],
        compiler_params=_params(),
        name="conv_mixer",
    )(x, x, x, g, w_in, cw, w_out)


def _mlp_kernel(x_ref, g_ref, wup_ref, wdn_ref, *rest, final_norm):
    o_ref, wup_bf, wdn_bf = rest[-3:]
    step = pl.program_id(0)
    up_rows, ff = wup_ref.shape
    dn_rows = wdn_ref.shape[0]

    @pl.when(step < CAST_STEPS)
    def _():
        wup_bf[pl.ds(pl.multiple_of(step * up_rows, up_rows), up_rows), :] = wup_ref[...].astype(BF16)
        wdn_bf[pl.ds(pl.multiple_of(step * dn_rows, dn_rows), dn_rows), :] = wdn_ref[...].astype(BF16)

    @pl.when(step >= CAST_STEPS)
    def _():
        x = x_ref[...]
        h = _rmsnorm(x, g_ref[...]).astype(BF16)
        acc = x
        for c in range(ff // FF_CHUNK):
            cols = slice(c * FF_CHUNK, (c + 1) * FF_CHUNK)
            a = jnp.maximum(jnp.dot(h, wup_bf[:, cols], preferred_element_type=F32), 0.0)
            acc = acc + jnp.dot((a * a).astype(BF16), wdn_bf[cols, :], preferred_element_type=F32)
        if final_norm:
            acc = _rmsnorm(acc, rest[0][...])
        o_ref[...] = acc


def _mlp(x, g, wup, wdn, layer, g_final=None):
    n, d = x.shape
    ff = wup.shape[2]
    tile = pl.BlockSpec((ROW_TILE, d), lambda s: (jnp.maximum(s - CAST_STEPS, 0), 0))

    def chunk(s):
        return (layer, jnp.minimum(s, CAST_STEPS - 1), 0)

    in_specs = [tile, _resident((1, d)), pl.BlockSpec((None, d // CAST_STEPS, ff), chunk),
                pl.BlockSpec((None, ff // CAST_STEPS, d), chunk)]
    args = [x, g, wup, wdn]
    if g_final is not None:
        in_specs.append(_resident((1, d)))
        args.append(g_final)
    return pl.pallas_call(
        functools.partial(_mlp_kernel, final_norm=g_final is not None),
        grid=(CAST_STEPS + n // ROW_TILE,),
        in_specs=in_specs,
        out_specs=tile,
        out_shape=jax.ShapeDtypeStruct((n, d), F32),
        scratch_shapes=[pltpu.VMEM((d, ff), BF16), pltpu.VMEM((ff, d), BF16)],
        compiler_params=_params(),
        name="mlp_final" if g_final is not None else "mlp",
    )(*args)


def _qkv_kernel(x_ref, g_ref, w_ref, qk_ref, vt_ref, w_bf):
    d = x_ref.shape[1]

    @pl.when(pl.program_id(0) == 0)
    def _():
        w_bf[...] = w_ref[...].astype(BF16)

    h = _rmsnorm(x_ref[...], g_ref[...]).astype(BF16)
    for t in range(3):
        u = jnp.dot(h, w_bf[:, t * d:(t + 1) * d], preferred_element_type=F32)
        if t == 0:
            u = u * (HEAD_DIM ** -0.5)
        for p in range(N_PAIRS):
            blk = u[:, p * LANES:(p + 1) * LANES]
            if t < 2:
                qk_ref[t, p] = blk.astype(BF16)
            else:
                for b in range(vt_ref.shape[0]):
                    vt_ref[b, p] = blk[b * KEY_BLOCK:(b + 1) * KEY_BLOCK].T.astype(BF16)


def _qkv(x, g, w):
    n, d = x.shape
    return pl.pallas_call(
        _qkv_kernel,
        grid=(n // ROW_TILE,),
        in_specs=[pl.BlockSpec((ROW_TILE, d), lambda i: (i, 0)), _resident((1, d)), _resident((d, 3 * d))],
        out_specs=[pl.BlockSpec((2, N_PAIRS, ROW_TILE, LANES), lambda i: (0, 0, i, 0)),
                   pl.BlockSpec((ROW_TILE // KEY_BLOCK, N_PAIRS, LANES, KEY_BLOCK), lambda i: (i, 0, 0, 0))],
        out_shape=[jax.ShapeDtypeStruct((2, N_PAIRS, n, LANES), BF16),
                   jax.ShapeDtypeStruct((n // KEY_BLOCK, N_PAIRS, LANES, KEY_BLOCK), BF16)],
        scratch_shapes=[pltpu.VMEM((d, 3 * d), BF16)],
        compiler_params=_params(),
        name="qkv",
    )(x, g, w)


def _bias_table_kernel(rpb_ref, o_ref):
    shape = (GRID_W, LANES)
    kc = lax.broadcasted_iota(jnp.int32, shape, 0)
    lane = lax.broadcasted_iota(jnp.int32, shape, 1)
    c = lane & (GRID_W - 1)
    col_start = jnp.clip(c - WIN_W // 2, 0, GRID_W - WIN_W)
    in_window = (kc >= col_start) & (kc < col_start + WIN_W)
    left = lane < GRID_W
    tiles = []
    for r in range(2 * WIN_H - 1):
        row = jnp.broadcast_to(rpb_ref[0, r:r + 1, :], shape)
        t_left = pltpu.roll(row, LANES - (WIN_W - 1), axis=1, stride=1, stride_axis=0)
        t_right = pltpu.roll(row, GRID_W - (WIN_W - 1), axis=1, stride=1, stride_axis=0)
        tiles.append(jnp.where(in_window, jnp.where(left, t_left, t_right), MASKED))
    for e in range(N_FULL):
        o_ref[0, e] = jnp.where(left, tiles[e + 1], tiles[e])
    o_ref[0, E_RIGHT_MASKED] = jnp.where(left, tiles[WIN_H - 1 - WIN_H // 2], MASKED)
    o_ref[0, E_LEFT_MASKED] = jnp.where(left, MASKED, tiles[WIN_H - 1 + WIN_H // 2 - 1])


def _bias_table(rpb):
    h, nr, nc = rpb.shape
    padded = jnp.pad(rpb[:, :, ::-1], ((0, 0), (0, 16 - nr), (0, LANES - nc)))
    return pl.pallas_call(
        _bias_table_kernel,
        grid=(h,),
        in_specs=[pl.BlockSpec((1, 16, LANES), lambda i: (i, 0, 0))],
        out_specs=pl.BlockSpec((1, N_ENTRIES, GRID_W, LANES), lambda i: (i, 0, 0, 0)),
        out_shape=jax.ShapeDtypeStruct((h, N_ENTRIES, GRID_W, LANES), F32),
        compiler_params=_params(),
        name="bias_table",
    )(padded)


def _window_rows(case, j):
    offset = (0, WIN_H // 2, WIN_H)[case]

    def in_window(rq, rk):
        first = (0, rq, WIN_H // 2)[case]
        return first <= rk < first + WIN_H

    out = []
    for rk in range(N_KEY_BLOCKS * KEY_BLOCK_ROWS):
        left_ok, right_ok = in_window(2 * j, rk), in_window(2 * j + 1, rk)
        dr_left = rk - 2 * j - offset
        if left_ok and right_ok:
            out.append((rk, dr_left + WIN_H - 2))
        elif left_ok:
            assert dr_left == -WIN_H // 2
            out.append((rk, E_RIGHT_MASKED))
        elif right_ok:
            assert dr_left - 1 == WIN_H // 2 - 1
            out.append((rk, E_LEFT_MASKED))
    return out


def _step_plan(step, steps_per_seq):
    blocks_per_seq = steps_per_seq * SUB_BLOCKS
    step_first = min(max(SUB_BLOCKS * step - 1, 0), blocks_per_seq - STEP_KEY_BLOCKS)
    plan = []
    for sub in range(SUB_BLOCKS):
        qb = SUB_BLOCKS * step + sub
        case = 0 if qb == 0 else 2 if qb == blocks_per_seq - 1 else 1
        plan.append((case, min(max(qb - 1, 0), blocks_per_seq - N_KEY_BLOCKS) - step_first))
    return plan


def _attn_kernel(q_ref, *rest, steps_per_seq, n_steps):
    k_refs, v_refs = rest[:STEP_KEY_BLOCKS], rest[STEP_KEY_BLOCKS:2 * STEP_KEY_BLOCKS]
    tbl_ref, x_ref, wo_ref, o_ref, obuf, wo_bf = rest[2 * STEP_KEY_BLOCKS:]
    mq = Q_ROWS * GRID_W
    t = pl.program_id(0)
    step = lax.rem(t, steps_per_seq)
    slot = lax.rem(t, 2)

    @pl.when(t == 0)
    def _():
        wo_bf[...] = wo_ref[...].astype(BF16)
        obuf[1] = jnp.zeros(obuf.shape[1:], BF16)

    last = steps_per_seq - 1
    n_key_rows = N_KEY_BLOCKS * KEY_BLOCK_ROWS
    left = lax.broadcasted_iota(jnp.int32, (mq, LANES), 1) < HEAD_DIM
    ones_rows = jnp.ones((ONES_ROWS, n_key_rows * GRID_W), BF16)

    def project_previous(part=None):
        a = jnp.concatenate([obuf[1 - slot, p] for p in range(N_PAIRS)], axis=1)
        cols = slice(None) if part is None else slice(part * PROJ_CHUNK, (part + 1) * PROJ_CHUNK)
        o_ref[:, cols] = x_ref[:, cols] + jnp.dot(a, wo_bf[:, cols], preferred_element_type=F32)

    def scores(p, sub, lo):
        q = q_ref[p, sub * mq:(sub + 1) * mq]
        q2 = jnp.concatenate([jnp.where(left, q, 0), jnp.where(left, 0, q)], axis=0)
        k = jnp.concatenate([k_refs[lo + i][p] for i in range(N_KEY_BLOCKS)], axis=0)
        return lax.dot_general(k, q2, (((1,), (1,)), ((), ())), preferred_element_type=F32)

    def finish(case, p, sub, lo, st):
        vt = jnp.concatenate([v_refs[lo + i][p] for i in range(N_KEY_BLOCKS)], axis=1)
        outs = []
        for hh in range(2):
            head = 2 * p + hh
            cols = []
            for j in range(Q_ROWS // 2):
                lt = 2 * hh + j
                rows = _window_rows(case, j)
                sc = [st[rk * GRID_W:(rk + 1) * GRID_W, lt * LANES:(lt + 1) * LANES] + tbl_ref[head, e]
                      for rk, e in rows]
                m = jnp.max(functools.reduce(jnp.maximum, sc), axis=0, keepdims=True)
                by_row = {rk: jnp.exp(x - m).astype(BF16) for (rk, _), x in zip(rows, sc)}
                cols.append(jnp.concatenate(
                    [by_row[rk] if rk in by_row else jnp.zeros((GRID_W, LANES), BF16)
                     for rk in range(n_key_rows)], axis=0))
            probs_t = jnp.concatenate(cols, axis=1)
            v_ones = jnp.concatenate([vt[hh * HEAD_DIM:(hh + 1) * HEAD_DIM], ones_rows], axis=0)
            o_t = jnp.dot(v_ones, probs_t, preferred_element_type=F32)
            outs.append(o_t[:HEAD_DIM] / o_t[HEAD_DIM:HEAD_DIM + 1])
        obuf[slot, p, sub * mq:(sub + 1) * mq] = jnp.concatenate(outs, axis=0).T.astype(BF16)

    assert all(_step_plan(s, steps_per_seq) == _step_plan(1, steps_per_seq) for s in range(1, last))
    for rep, cond in ((0, step == 0), (1, (step > 0) & (step < last)), (last, step == last)):
        @pl.when(cond & (t < n_steps))
        def _(plan=_step_plan(rep, steps_per_seq)):
            units = [(p, sub) for p in range(N_PAIRS) for sub in range(SUB_BLOCKS)]
            every = len(units) * PROJ_CHUNK // (N_HEADS * HEAD_DIM)
            st = scores(*units[0], plan[units[0][1]][1])
            for u, (p, sub) in enumerate(units):
                st_next = None
                if u + 1 < len(units):
                    p_next, sub_next = units[u + 1]
                    st_next = scores(p_next, sub_next, plan[sub_next][1])
                if u % every == every - 1:
                    project_previous(u // every)
                finish(plan[sub][0], p, sub, plan[sub][1], st)
                st = st_next

    pl.when(t == n_steps)(project_previous)


def _attn(qk, vt, tbl, x, wo, seq):
    n, d = x.shape
    mq = SUB_BLOCKS * Q_ROWS * GRID_W
    kb = KEY_BLOCK
    steps = seq // mq
    n_steps = n // mq
    last_start = seq // kb - STEP_KEY_BLOCKS

    def first_key_block(t):
        blk = jnp.minimum(t, n_steps - 1)
        return (blk // steps) * (seq // kb) + jnp.clip(SUB_BLOCKS * (blk % steps) - 1, 0, last_start)

    tile = pl.BlockSpec((mq, d), lambda t: (jnp.maximum(t - 1, 0), 0))
    in_specs = [pl.BlockSpec((None, N_PAIRS, mq, LANES), lambda t: (0, 0, jnp.minimum(t, n_steps - 1), 0))]
    in_specs += [pl.BlockSpec((None, N_PAIRS, kb, LANES), lambda t, j=j: (1, 0, first_key_block(t) + j, 0))
                 for j in range(STEP_KEY_BLOCKS)]
    in_specs += [pl.BlockSpec((None, N_PAIRS, LANES, kb), lambda t, j=j: (first_key_block(t) + j, 0, 0, 0))
                 for j in range(STEP_KEY_BLOCKS)]
    in_specs += [_resident(tbl.shape), tile, _resident((d, d))]
    return pl.pallas_call(
        functools.partial(_attn_kernel, steps_per_seq=steps, n_steps=n_steps),
        grid=(n_steps + 1,),
        in_specs=in_specs,
        out_specs=tile,
        out_shape=jax.ShapeDtypeStruct((n, d), F32),
        scratch_shapes=[pltpu.VMEM((2, N_PAIRS, mq, LANES), BF16), pltpu.VMEM((d, d), BF16)],
        compiler_params=_params(),
        name="natten",
    )(*([qk] * (1 + STEP_KEY_BLOCKS)), *([vt] * STEP_KEY_BLOCKS), tbl, x, wo)


def kernel(x, norm_mix, conv_w_in, conv_w, conv_w_out, attn_w_qkv, attn_rpb, attn_w_o, norm_mlp, mlp_w_up,
           mlp_w_down, norm_final):
    batch, seq, d = x.shape
    n = batch * seq
    assert d == N_HEADS * HEAD_DIM and seq % (GRID_W * Q_ROWS * SUB_BLOCKS) == 0 and d % PROJ_CHUNK == 0
    assert seq // KEY_BLOCK >= STEP_KEY_BLOCKS
    assert seq % ROW_TILE == 0 and ROW_TILE % HALO == 0 and d % CONV_CHUNK == 0 and d % (8 * CAST_STEPS) == 0
    assert N_KEY_BLOCKS == 3 and Q_ROWS == KEY_BLOCK_ROWS == WIN_H // 2 and ROW_TILE % KEY_BLOCK == 0
    assert norm_mix.shape[0] == 2 and conv_w_in.shape[0] == 1 and attn_w_qkv.shape[0] == 1

    xf = x.reshape(n, d)
    xf = _conv_mixer(xf, norm_mix[0:1], conv_w_in[0], conv_w[0], conv_w_out[0], seq)
    xf = _mlp(xf, norm_mlp[0:1], mlp_w_up, mlp_w_down, 0)

    qk, vt = _qkv(xf, norm_mix[1:2], attn_w_qkv[0])
    tbl = _bias_table(attn_rpb[0])
    xf = _attn(qk, vt, tbl, xf, attn_w_o[0], seq)
    out = _mlp(xf, norm_mlp[1:2], mlp_w_up, mlp_w_down, 1, norm_final.reshape(1, d))
    return out.reshape(batch, seq, d)
```

```python
import functools

import jax
import jax.numpy as jnp
from jax import lax
from jax.experimental import pallas as pl
from jax.experimental.pallas import tpu as pltpu

F32 = jnp.float32
BF16 = jnp.bfloat16

NORM_EPS = 1e-6
N_HEADS = 16
HEAD_DIM = 64
GRID_W = 64
WIN_H = 8
WIN_W = 16
LANES = 128
N_PAIRS = N_HEADS * HEAD_DIM // LANES

ROW_TILE = 1024
FF_CHUNK = 512
HALO = 16
CONV_CHUNK = 256
Q_ROWS = 4
KEY_BLOCK_ROWS = 4
KEY_BLOCK = KEY_BLOCK_ROWS * GRID_W
N_KEY_BLOCKS = Q_ROWS // KEY_BLOCK_ROWS + WIN_H // KEY_BLOCK_ROWS
SUB_BLOCKS = 2
STEP_KEY_BLOCKS = N_KEY_BLOCKS + SUB_BLOCKS - 1
MASKED = -1e30
PROJ_CHUNK = 256

N_FULL = 2 * WIN_H - 2
E_RIGHT_MASKED = N_FULL
E_LEFT_MASKED = N_FULL + 1
N_ENTRIES = N_FULL + 2
ONES_ROWS = 16

VMEM_LIMIT = 52 * 1024 * 1024


def _params():
    return pltpu.CompilerParams(dimension_semantics=("arbitrary",), vmem_limit_bytes=VMEM_LIMIT)


def _resident(shape):
    return pl.BlockSpec(shape, lambda *_: (0,) * len(shape), pipeline_mode=pl.Buffered(1))


def _rmsnorm(x, g):
    ms = jnp.mean(x * x, axis=-1, keepdims=True)
    return x * lax.rsqrt(ms + NORM_EPS) * g


def _cast_specs(wup, wdn, layer, n_slabs, slab_of):
    _, d, ff = wup.shape
    in_specs = [pl.BlockSpec((None, d // n_slabs, ff), lambda s: (layer, slab_of(s), 0)),
                pl.BlockSpec((None, ff // n_slabs, d), lambda s: (layer, slab_of(s), 0))]
    out_specs = [pl.BlockSpec((d // n_slabs, ff), lambda s: (slab_of(s), 0)),
                 pl.BlockSpec((ff // n_slabs, d), lambda s: (slab_of(s), 0))]
    out_shape = [jax.ShapeDtypeStruct((d, ff), BF16), jax.ShapeDtypeStruct((ff, d), BF16)]
    return in_specs, out_specs, out_shape


def _conv_kernel(x_ref, xprev_ref, xnext_ref, g_ref, win_ref, cw_ref, wout_ref, wup_ref, wdn_ref, o_ref, wup_o,
                 wdn_o, hbuf, ybuf, win_bf, wout_bf, *, tiles_per_seq):
    tm, d = x_ref.shape
    rows = tm + 2 * HALO
    step = pl.program_id(0)
    wup_o[...] = wup_ref[...].astype(BF16)
    wdn_o[...] = wdn_ref[...].astype(BF16)

    @pl.when(step == 0)
    def _():
        win_bf[...] = win_ref[...].astype(BF16)
        wout_bf[...] = wout_ref[...].astype(BF16)

    pos = lax.rem(step, tiles_per_seq)
    g = g_ref[...]
    hbuf[0:HALO] = _rmsnorm(xprev_ref[...], g).astype(BF16)
    hbuf[HALO:HALO + tm] = _rmsnorm(x_ref[...], g).astype(BF16)
    hbuf[HALO + tm:rows] = _rmsnorm(xnext_ref[...], g).astype(BF16)
    h = hbuf[...]
    row = lax.broadcasted_iota(jnp.int32, (rows, 1), 0)
    outside = ((row < HALO) & (pos == 0)) | ((row >= HALO + tm) & (pos == tiles_per_seq - 1))
    for j in range(d // CONV_CHUNK):
        cols = slice(j * CONV_CHUNK, (j + 1) * CONV_CHUNK)
        c = jnp.dot(h, win_bf[:, d + j * CONV_CHUNK:d + (j + 1) * CONV_CHUNK], preferred_element_type=F32)
        v = jnp.dot(h, win_bf[:, 2 * d + j * CONV_CHUNK:2 * d + (j + 1) * CONV_CHUNK], preferred_element_type=F32)
        z = jnp.where(outside, 0.0, c * v)
        z_m1 = pltpu.roll(z, 1, axis=0)[HALO:HALO + tm]
        z_p1 = pltpu.roll(z, rows - 1, axis=0)[HALO:HALO + tm]
        zc = cw_ref[0:1, cols] * z_m1 + cw_ref[1:2, cols] * z[HALO:HALO + tm] + cw_ref[2:3, cols] * z_p1
        gate = jnp.dot(hbuf[HALO:HALO + tm], win_bf[:, cols], preferred_element_type=F32)
        ybuf[:, cols] = (gate * zc).astype(BF16)
    o_ref[...] = x_ref[...] + jnp.dot(ybuf[...], wout_bf[...], preferred_element_type=F32)


def _conv_mixer(x, g, w_in, cw, w_out, wup, wdn, seq):
    n, d = x.shape
    cast_in, cast_out, cast_shape = _cast_specs(wup, wdn, 0, n // ROW_TILE, lambda i: i)
    per = ROW_TILE // HALO
    tile = pl.BlockSpec((ROW_TILE, d), lambda i: (i, 0))
    prev_spec = pl.BlockSpec((HALO, d), lambda i: (jnp.maximum(i * per - 1, 0), 0))
    next_spec = pl.BlockSpec((HALO, d), lambda i: (jnp.minimum((i + 1) * per, n // HALO - 1), 0))
    return pl.pallas_call(
        functools.partial(_conv_kernel, tiles_per_seq=seq // ROW_TILE),
        grid=(n // ROW_TILE,),
        in_specs=[tile, prev_spec, next_spec, _resident((1, d)), _resident((d, 3 * d)), _resident(cw.shape),
                  _resident((d, d))] + cast_in,
        out_specs=[tile] + cast_out,
        out_shape=[jax.ShapeDtypeStruct((n, d), F32)] + cast_shape,
        scratch_shapes=[pltpu.VMEM((ROW_TILE + 2 * HALO, d), BF16), pltpu.VMEM((ROW_TILE, d), BF16),
                        pltpu.VMEM((d, 3 * d), BF16), pltpu.VMEM((d, d), BF16)],
        compiler_params=_params(),
        name="conv_mixer",
    )(x, x, x, g, w_in, cw, w_out, wup, wdn)


def _mlp_kernel(x_ref, g_ref, wup_ref, wdn_ref, *rest, final_norm):
    o_ref = rest[-1]
    ff = wup_ref.shape[1]
    x = x_ref[...]
    h = _rmsnorm(x, g_ref[...]).astype(BF16)
    acc = x
    for c in range(ff // FF_CHUNK):
        cols = slice(c * FF_CHUNK, (c + 1) * FF_CHUNK)
        a = jnp.maximum(jnp.dot(h, wup_ref[:, cols], preferred_element_type=F32), 0.0)
        acc = acc + jnp.dot((a * a).astype(BF16), wdn_ref[cols, :], preferred_element_type=F32)
    if final_norm:
        acc = _rmsnorm(acc, rest[0][...])
    o_ref[...] = acc


def _mlp(x, g, wup, wdn, g_final=None):
    n, d = x.shape
    ff = wup.shape[1]
    tile = pl.BlockSpec((ROW_TILE, d), lambda i: (i, 0))
    in_specs = [tile, _resident((1, d)), _resident((d, ff)), _resident((ff, d))]
    args = [x, g, wup, wdn]
    if g_final is not None:
        in_specs.append(_resident((1, d)))
        args.append(g_final)
    return pl.pallas_call(
        functools.partial(_mlp_kernel, final_norm=g_final is not None),
        grid=(n // ROW_TILE,),
        in_specs=in_specs,
        out_specs=tile,
        out_shape=jax.ShapeDtypeStruct((n, d), F32),
        compiler_params=_params(),
        name="mlp_final" if g_final is not None else "mlp",
    )(*args)


def _qkv_kernel(x_ref, g_ref, w_ref, qk_ref, vt_ref, w_bf):
    d = x_ref.shape[1]

    @pl.when(pl.program_id(0) == 0)
    def _():
        w_bf[...] = w_ref[...].astype(BF16)

    h = _rmsnorm(x_ref[...], g_ref[...]).astype(BF16)
    for t in range(3):
        u = jnp.dot(h, w_bf[:, t * d:(t + 1) * d], preferred_element_type=F32)
        if t == 0:
            u = u * (HEAD_DIM ** -0.5)
        for p in range(N_PAIRS):
            blk = u[:, p * LANES:(p + 1) * LANES]
            if t < 2:
                qk_ref[t, p] = blk.astype(BF16)
            else:
                for b in range(vt_ref.shape[0]):
                    vt_ref[b, p] = blk[b * KEY_BLOCK:(b + 1) * KEY_BLOCK].T.astype(BF16)


def _qkv(x, g, w):
    n, d = x.shape
    return pl.pallas_call(
        _qkv_kernel,
        grid=(n // ROW_TILE,),
        in_specs=[pl.BlockSpec((ROW_TILE, d), lambda i: (i, 0)), _resident((1, d)), _resident((d, 3 * d))],
        out_specs=[pl.BlockSpec((2, N_PAIRS, ROW_TILE, LANES), lambda i: (0, 0, i, 0)),
                   pl.BlockSpec((ROW_TILE // KEY_BLOCK, N_PAIRS, LANES, KEY_BLOCK), lambda i: (i, 0, 0, 0))],
        out_shape=[jax.ShapeDtypeStruct((2, N_PAIRS, n, LANES), BF16),
                   jax.ShapeDtypeStruct((n // KEY_BLOCK, N_PAIRS, LANES, KEY_BLOCK), BF16)],
        scratch_shapes=[pltpu.VMEM((d, 3 * d), BF16)],
        compiler_params=_params(),
        name="qkv",
    )(x, g, w)


def _bias_table_kernel(rpb_ref, o_ref):
    shape = (GRID_W, LANES)
    kc = lax.broadcasted_iota(jnp.int32, shape, 0)
    lane = lax.broadcasted_iota(jnp.int32, shape, 1)
    c = lane & (GRID_W - 1)
    col_start = jnp.clip(c - WIN_W // 2, 0, GRID_W - WIN_W)
    in_window = (kc >= col_start) & (kc < col_start + WIN_W)
    left = lane < GRID_W
    tiles = []
    for r in range(2 * WIN_H - 1):
        row = jnp.broadcast_to(rpb_ref[0, r:r + 1, :], shape)
        t_left = pltpu.roll(row, LANES - (WIN_W - 1), axis=1, stride=1, stride_axis=0)
        t_right = pltpu.roll(row, GRID_W - (WIN_W - 1), axis=1, stride=1, stride_axis=0)
        tiles.append(jnp.where(in_window, jnp.where(left, t_left, t_right), MASKED))
    for e in range(N_FULL):
        o_ref[0, e] = jnp.where(left, tiles[e + 1], tiles[e])
    o_ref[0, E_RIGHT_MASKED] = jnp.where(left, tiles[WIN_H - 1 - WIN_H // 2], MASKED)
    o_ref[0, E_LEFT_MASKED] = jnp.where(left, MASKED, tiles[WIN_H - 1 + WIN_H // 2 - 1])


def _bias_table(rpb):
    h, nr, nc = rpb.shape
    padded = jnp.pad(rpb[:, :, ::-1], ((0, 0), (0, 16 - nr), (0, LANES - nc)))
    return pl.pallas_call(
        _bias_table_kernel,
        grid=(h,),
        in_specs=[pl.BlockSpec((1, 16, LANES), lambda i: (i, 0, 0))],
        out_specs=pl.BlockSpec((1, N_ENTRIES, GRID_W, LANES), lambda i: (i, 0, 0, 0)),
        out_shape=jax.ShapeDtypeStruct((h, N_ENTRIES, GRID_W, LANES), F32),
        compiler_params=_params(),
        name="bias_table",
    )(padded)


def _window_rows(case, j):
    offset = (0, WIN_H // 2, WIN_H)[case]

    def in_window(rq, rk):
        first = (0, rq, WIN_H // 2)[case]
        return first <= rk < first + WIN_H

    out = []
    for rk in range(N_KEY_BLOCKS * KEY_BLOCK_ROWS):
        left_ok, right_ok = in_window(2 * j, rk), in_window(2 * j + 1, rk)
        dr_left = rk - 2 * j - offset
        if left_ok and right_ok:
            out.append((rk, dr_left + WIN_H - 2))
        elif left_ok:
            assert dr_left == -WIN_H // 2
            out.append((rk, E_RIGHT_MASKED))
        elif right_ok:
            assert dr_left - 1 == WIN_H // 2 - 1
            out.append((rk, E_LEFT_MASKED))
    return out


def _step_plan(step, steps_per_seq):
    blocks_per_seq = steps_per_seq * SUB_BLOCKS
    step_first = min(max(SUB_BLOCKS * step - 1, 0), blocks_per_seq - STEP_KEY_BLOCKS)
    plan = []
    for sub in range(SUB_BLOCKS):
        qb = SUB_BLOCKS * step + sub
        case = 0 if qb == 0 else 2 if qb == blocks_per_seq - 1 else 1
        plan.append((case, min(max(qb - 1, 0), blocks_per_seq - N_KEY_BLOCKS) - step_first))
    return plan


def _attn_kernel(q_ref, *rest, steps_per_seq, n_steps):
    k_refs, v_refs = rest[:STEP_KEY_BLOCKS], rest[STEP_KEY_BLOCKS:2 * STEP_KEY_BLOCKS]
    tbl_ref, x_ref, wo_ref, wup_ref, wdn_ref, o_ref, wup_o, wdn_o, obuf, wo_bf = rest[2 * STEP_KEY_BLOCKS:]
    mq = Q_ROWS * GRID_W
    t = pl.program_id(0)
    wup_o[...] = wup_ref[...].astype(BF16)
    wdn_o[...] = wdn_ref[...].astype(BF16)
    step = lax.rem(t, steps_per_seq)
    slot = lax.rem(t, 2)

    @pl.when(t == 0)
    def _():
        wo_bf[...] = wo_ref[...].astype(BF16)
        obuf[1] = jnp.zeros(obuf.shape[1:], BF16)

    last = steps_per_seq - 1
    n_key_rows = N_KEY_BLOCKS * KEY_BLOCK_ROWS
    left = lax.broadcasted_iota(jnp.int32, (mq, LANES), 1) < HEAD_DIM
    ones_rows = jnp.ones((ONES_ROWS, n_key_rows * GRID_W), BF16)

    def project_previous(part=None):
        a = jnp.concatenate([obuf[1 - slot, p] for p in range(N_PAIRS)], axis=1)
        cols = slice(None) if part is None else slice(part * PROJ_CHUNK, (part + 1) * PROJ_CHUNK)
        o_ref[:, cols] = x_ref[:, cols] + jnp.dot(a, wo_bf[:, cols], preferred_element_type=F32)

    def scores(p, sub, lo):
        q = q_ref[p, sub * mq:(sub + 1) * mq]
        q2 = jnp.concatenate([jnp.where(left, q, 0), jnp.where(left, 0, q)], axis=0)
        k = jnp.concatenate([k_refs[lo + i][p] for i in range(N_KEY_BLOCKS)], axis=0)
        return lax.dot_general(k, q2, (((1,), (1,)), ((), ())), preferred_element_type=F32)

    def finish(case, p, sub, lo, st):
        vt = jnp.concatenate([v_refs[lo + i][p] for i in range(N_KEY_BLOCKS)], axis=1)
        outs = []
        for hh in range(2):
            head = 2 * p + hh
            cols = []
            for j in range(Q_ROWS // 2):
                lt = 2 * hh + j
                rows = _window_rows(case, j)
                sc = [st[rk * GRID_W:(rk + 1) * GRID_W, lt * LANES:(lt + 1) * LANES] + tbl_ref[head, e]
                      for rk, e in rows]
                m = jnp.max(functools.reduce(jnp.maximum, sc), axis=0, keepdims=True)
                by_row = {rk: jnp.exp(x - m).astype(BF16) for (rk, _), x in zip(rows, sc)}
                cols.append(jnp.concatenate(
                    [by_row[rk] if rk in by_row else jnp.zeros((GRID_W, LANES), BF16)
                     for rk in range(n_key_rows)], axis=0))
            probs_t = jnp.concatenate(cols, axis=1)
            v_ones = jnp.concatenate([vt[hh * HEAD_DIM:(hh + 1) * HEAD_DIM], ones_rows], axis=0)
            o_t = jnp.dot(v_ones, probs_t, preferred_element_type=F32)
            outs.append(o_t[:HEAD_DIM] / o_t[HEAD_DIM:HEAD_DIM + 1])
        obuf[slot, p, sub * mq:(sub + 1) * mq] = jnp.concatenate(outs, axis=0).T.astype(BF16)

    assert all(_step_plan(s, steps_per_seq) == _step_plan(1, steps_per_seq) for s in range(1, last))
    for rep, cond in ((0, step == 0), (1, (step > 0) & (step < last)), (last, step == last)):
        @pl.when(cond & (t < n_steps))
        def _(plan=_step_plan(rep, steps_per_seq)):
            units = [(p, sub) for p in range(N_PAIRS) for sub in range(SUB_BLOCKS)]
            every = len(units) * PROJ_CHUNK // (N_HEADS * HEAD_DIM)
            st = scores(*units[0], plan[units[0][1]][1])
            for u, (p, sub) in enumerate(units):
                st_next = None
                if u + 1 < len(units):
                    p_next, sub_next = units[u + 1]
                    st_next = scores(p_next, sub_next, plan[sub_next][1])
                if u % every == every - 1:
                    project_previous(u // every)
                finish(plan[sub][0], p, sub, plan[sub][1], st)
                st = st_next

    pl.when(t == n_steps)(project_previous)


def _attn(qk, vt, tbl, x, wo, wup, wdn, seq):
    n, d = x.shape
    mq = SUB_BLOCKS * Q_ROWS * GRID_W
    kb = KEY_BLOCK
    steps = seq // mq
    n_steps = n // mq
    last_start = seq // kb - STEP_KEY_BLOCKS

    def first_key_block(t):
        blk = jnp.minimum(t, n_steps - 1)
        return (blk // steps) * (seq // kb) + jnp.clip(SUB_BLOCKS * (blk % steps) - 1, 0, last_start)

    tile = pl.BlockSpec((mq, d), lambda t: (jnp.maximum(t - 1, 0), 0))
    in_specs = [pl.BlockSpec((None, N_PAIRS, mq, LANES), lambda t: (0, 0, jnp.minimum(t, n_steps - 1), 0))]
    in_specs += [pl.BlockSpec((None, N_PAIRS, kb, LANES), lambda t, j=j: (1, 0, first_key_block(t) + j, 0))
                 for j in range(STEP_KEY_BLOCKS)]
    in_specs += [pl.BlockSpec((None, N_PAIRS, LANES, kb), lambda t, j=j: (first_key_block(t) + j, 0, 0, 0))
                 for j in range(STEP_KEY_BLOCKS)]
    cast_in, cast_out, cast_shape = _cast_specs(wup, wdn, 1, n_steps, lambda t: jnp.minimum(t, n_steps - 1))
    in_specs += [_resident(tbl.shape), tile, _resident((d, d))] + cast_in
    return pl.pallas_call(
        functools.partial(_attn_kernel, steps_per_seq=steps, n_steps=n_steps),
        grid=(n_steps + 1,),
        in_specs=in_specs,
        out_specs=[tile] + cast_out,
        out_shape=[jax.ShapeDtypeStruct((n, d), F32)] + cast_shape,
        scratch_shapes=[pltpu.VMEM((2, N_PAIRS, mq, LANES), BF16), pltpu.VMEM((d, d), BF16)],
        compiler_params=_params(),
        name="natten",
    )(*([qk] * (1 + STEP_KEY_BLOCKS)), *([vt] * STEP_KEY_BLOCKS), tbl, x, wo, wup, wdn)


def kernel(x, norm_mix, conv_w_in, conv_w, conv_w_out, attn_w_qkv, attn_rpb, attn_w_o, norm_mlp, mlp_w_up,
           mlp_w_down, norm_final):
    batch, seq, d = x.shape
    n = batch * seq
    assert d == N_HEADS * HEAD_DIM and seq % (GRID_W * Q_ROWS * SUB_BLOCKS) == 0 and d % PROJ_CHUNK == 0
    assert seq // KEY_BLOCK >= STEP_KEY_BLOCKS
    assert seq % ROW_TILE == 0 and ROW_TILE % HALO == 0 and d % CONV_CHUNK == 0
    assert N_KEY_BLOCKS == 3 and Q_ROWS == KEY_BLOCK_ROWS == WIN_H // 2 and ROW_TILE % KEY_BLOCK == 0
    assert norm_mix.shape[0] == 2 and conv_w_in.shape[0] == 1 and attn_w_qkv.shape[0] == 1

    xf = x.reshape(n, d)
    xf, wup, wdn = _conv_mixer(xf, norm_mix[0:1], conv_w_in[0], conv_w[0], conv_w_out[0], mlp_w_up, mlp_w_down, seq)
    xf = _mlp(xf, norm_mlp[0:1], wup, wdn)

    qk, vt = _qkv(xf, norm_mix[1:2], attn_w_qkv[0])
    tbl = _bias_table(attn_rpb[0])
    xf, wup, wdn = _attn(qk, vt, tbl, xf, attn_w_o[0], mlp_w_up, mlp_w_down, seq)
    out = _mlp(xf, norm_mlp[1:2], wup, wdn, norm_final.reshape(1, d))
    return out.reshape(batch, seq, d)
```

```python
import functools

import jax
import jax.numpy as jnp
from jax import lax
from jax.experimental import pallas as pl
from jax.experimental.pallas import tpu as pltpu

F32 = jnp.float32
BF16 = jnp.bfloat16

NORM_EPS = 1e-6
N_HEADS = 16
HEAD_DIM = 64
GRID_W = 64
WIN_H = 8
WIN_W = 16
LANES = 128
N_PAIRS = N_HEADS * HEAD_DIM // LANES

ROW_TILE = 1024
FF_CHUNK = 512
HALO = 16
CONV_CHUNK = 256
Q_ROWS = 4
KEY_BLOCK_ROWS = 4
KEY_BLOCK = KEY_BLOCK_ROWS * GRID_W
N_KEY_BLOCKS = Q_ROWS // KEY_BLOCK_ROWS + WIN_H // KEY_BLOCK_ROWS
SUB_BLOCKS = 2
STEP_KEY_BLOCKS = N_KEY_BLOCKS + SUB_BLOCKS - 1
MASKED = -1e30
PROJ_CHUNK = 256

N_FULL = 2 * WIN_H - 2
E_RIGHT_MASKED = N_FULL
E_LEFT_MASKED = N_FULL + 1
N_ENTRIES = N_FULL + 2
ONES_ROWS = 16

VMEM_LIMIT = 52 * 1024 * 1024


def _params():
    return pltpu.CompilerParams(dimension_semantics=("arbitrary",), vmem_limit_bytes=VMEM_LIMIT)


def _resident(shape):
    return pl.BlockSpec(shape, lambda *_: (0,) * len(shape), pipeline_mode=pl.Buffered(1))


def _rmsnorm(x, g):
    ms = jnp.mean(x * x, axis=-1, keepdims=True)
    return x * lax.rsqrt(ms + NORM_EPS) * g


def _cast_specs(wup, wdn, layer, n_slabs, slab_of):
    _, d, ff = wup.shape
    in_specs = [pl.BlockSpec((None, d // n_slabs, ff), lambda s: (layer, slab_of(s), 0)),
                pl.BlockSpec((None, ff // n_slabs, d), lambda s: (layer, slab_of(s), 0))]
    out_specs = [pl.BlockSpec((d // n_slabs, ff), lambda s: (slab_of(s), 0)),
                 pl.BlockSpec((ff // n_slabs, d), lambda s: (slab_of(s), 0))]
    out_shape = [jax.ShapeDtypeStruct((d, ff), BF16), jax.ShapeDtypeStruct((ff, d), BF16)]
    return in_specs, out_specs, out_shape


def _conv_kernel(x_ref, xprev_ref, xnext_ref, g_ref, win_ref, cw_ref, wout_ref, wup_ref, wdn_ref, o_ref, wup_o,
                 wdn_o, hbuf, ybuf, win_bf, wout_bf, *, tiles_per_seq):
    tm, d = x_ref.shape
    rows = tm + 2 * HALO
    step = pl.program_id(0)
    wup_o[...] = wup_ref[...].astype(BF16)
    wdn_o[...] = wdn_ref[...].astype(BF16)

    @pl.when(step == 0)
    def _():
        win_bf[...] = win_ref[...].astype(BF16)
        wout_bf[...] = wout_ref[...].astype(BF16)

    pos = lax.rem(step, tiles_per_seq)
    g = g_ref[...]
    hbuf[0:HALO] = _rmsnorm(xprev_ref[...], g).astype(BF16)
    hbuf[HALO:HALO + tm] = _rmsnorm(x_ref[...], g).astype(BF16)
    hbuf[HALO + tm:rows] = _rmsnorm(xnext_ref[...], g).astype(BF16)
    h = hbuf[...]
    row = lax.broadcasted_iota(jnp.int32, (rows, 1), 0)
    outside = ((row < HALO) & (pos == 0)) | ((row >= HALO + tm) & (pos == tiles_per_seq - 1))
    for j in range(d // CONV_CHUNK):
        cols = slice(j * CONV_CHUNK, (j + 1) * CONV_CHUNK)
        c = jnp.dot(h, win_bf[:, d + j * CONV_CHUNK:d + (j + 1) * CONV_CHUNK], preferred_element_type=F32)
        v = jnp.dot(h, win_bf[:, 2 * d + j * CONV_CHUNK:2 * d + (j + 1) * CONV_CHUNK], preferred_element_type=F32)
        z = jnp.where(outside, 0.0, c * v)
        z_m1 = pltpu.roll(z, 1, axis=0)[HALO:HALO + tm]
        z_p1 = pltpu.roll(z, rows - 1, axis=0)[HALO:HALO + tm]
        zc = cw_ref[0:1, cols] * z_m1 + cw_ref[1:2, cols] * z[HALO:HALO + tm] + cw_ref[2:3, cols] * z_p1
        gate = jnp.dot(hbuf[HALO:HALO + tm], win_bf[:, cols], preferred_element_type=F32)
        ybuf[:, cols] = (gate * zc).astype(BF16)
    o_ref[...] = x_ref[...] + jnp.dot(ybuf[...], wout_bf[...], preferred_element_type=F32)


def _conv_mixer(x, g, w_in, cw, w_out, wup, wdn, seq):
    n, d = x.shape
    cast_in, cast_out, cast_shape = _cast_specs(wup, wdn, 0, n // ROW_TILE, lambda i: i)
    per = ROW_TILE // HALO
    tile = pl.BlockSpec((ROW_TILE, d), lambda i: (i, 0))
    prev_spec = pl.BlockSpec((HALO, d), lambda i: (jnp.maximum(i * per - 1, 0), 0))
    next_spec = pl.BlockSpec((HALO, d), lambda i: (jnp.minimum((i + 1) * per, n // HALO - 1), 0))
    return pl.pallas_call(
        functools.partial(_conv_kernel, tiles_per_seq=seq // ROW_TILE),
        grid=(n // ROW_TILE,),
        in_specs=[tile, prev_spec, next_spec, _resident((1, d)), _resident((d, 3 * d)), _resident(cw.shape),
                  _resident((d, d))] + cast_in,
        out_specs=[tile] + cast_out,
        out_shape=[jax.ShapeDtypeStruct((n, d), F32)] + cast_shape,
        scratch_shapes=[pltpu.VMEM((ROW_TILE + 2 * HALO, d), BF16), pltpu.VMEM((ROW_TILE, d), BF16),
                        pltpu.VMEM((d, 3 * d), BF16), pltpu.VMEM((d, d), BF16)],
        compiler_params=_params(),
        name="conv_mixer",
    )(x, x, x, g, w_in, cw, w_out, wup, wdn)


def _mlp_kernel(x_ref, g_ref, wup_ref, wdn_ref, *rest, final_norm):
    o_ref = rest[-1]
    ff = wup_ref.shape[1]
    x = x_ref[...]
    h = _rmsnorm(x, g_ref[...]).astype(BF16)
    acc = x
    for c in range(ff // FF_CHUNK):
        cols = slice(c * FF_CHUNK, (c + 1) * FF_CHUNK)
        a = jnp.maximum(jnp.dot(h, wup_ref[:, cols], preferred_element_type=F32), 0.0)
        acc = acc + jnp.dot((a * a).astype(BF16), wdn_ref[cols, :], preferred_element_type=F32)
    if final_norm:
        acc = _rmsnorm(acc, rest[0][...])
    o_ref[...] = acc


def _mlp(x, g, wup, wdn, g_final=None):
    n, d = x.shape
    ff = wup.shape[1]
    tile = pl.BlockSpec((ROW_TILE, d), lambda i: (i, 0))
    in_specs = [tile, _resident((1, d)), _resident((d, ff)), _resident((ff, d))]
    args = [x, g, wup, wdn]
    if g_final is not None:
        in_specs.append(_resident((1, d)))
        args.append(g_final)
    return pl.pallas_call(
        functools.partial(_mlp_kernel, final_norm=g_final is not None),
        grid=(n // ROW_TILE,),
        in_specs=in_specs,
        out_specs=tile,
        out_shape=jax.ShapeDtypeStruct((n, d), F32),
        compiler_params=_params(),
        name="mlp_final" if g_final is not None else "mlp",
    )(*args)


def _qkv_kernel(x_ref, g_ref, w_ref, qk_ref, vt_ref, w_bf):
    d = x_ref.shape[1]

    @pl.when(pl.program_id(0) == 0)
    def _():
        w_bf[...] = w_ref[...].astype(BF16)

    h = _rmsnorm(x_ref[...], g_ref[...]).astype(BF16)
    for t in range(3):
        u = jnp.dot(h, w_bf[:, t * d:(t + 1) * d], preferred_element_type=F32)
        if t == 0:
            u = u * (HEAD_DIM ** -0.5)
        for p in range(N_PAIRS):
            blk = u[:, p * LANES:(p + 1) * LANES]
            if t < 2:
                qk_ref[t, p] = blk.astype(BF16)
            else:
                for b in range(vt_ref.shape[0]):
                    vt_ref[b, p] = blk[b * KEY_BLOCK:(b + 1) * KEY_BLOCK].T.astype(BF16)


def _qkv(x, g, w):
    n, d = x.shape
    return pl.pallas_call(
        _qkv_kernel,
        grid=(n // ROW_TILE,),
        in_specs=[pl.BlockSpec((ROW_TILE, d), lambda i: (i, 0)), _resident((1, d)), _resident((d, 3 * d))],
        out_specs=[pl.BlockSpec((2, N_PAIRS, ROW_TILE, LANES), lambda i: (0, 0, i, 0)),
                   pl.BlockSpec((ROW_TILE // KEY_BLOCK, N_PAIRS, LANES, KEY_BLOCK), lambda i: (i, 0, 0, 0))],
        out_shape=[jax.ShapeDtypeStruct((2, N_PAIRS, n, LANES), BF16),
                   jax.ShapeDtypeStruct((n // KEY_BLOCK, N_PAIRS, LANES, KEY_BLOCK), BF16)],
        scratch_shapes=[pltpu.VMEM((d, 3 * d), BF16)],
        compiler_params=_params(),
        name="qkv",
    )(x, g, w)


def _fill_bias_table(rpb_ref, tbl_ref, head):
    shape = (GRID_W, LANES)
    kc = lax.broadcasted_iota(jnp.int32, shape, 0)
    lane = lax.broadcasted_iota(jnp.int32, shape, 1)
    c = lane & (GRID_W - 1)
    col_start = jnp.clip(c - WIN_W // 2, 0, GRID_W - WIN_W)
    in_window = (kc >= col_start) & (kc < col_start + WIN_W)
    left = lane < GRID_W
    tiles = []
    for r in range(2 * WIN_H - 1):
        row = jnp.broadcast_to(rpb_ref[head, r:r + 1, :], shape)
        t_left = pltpu.roll(row, LANES - (WIN_W - 1), axis=1, stride=1, stride_axis=0)
        t_right = pltpu.roll(row, GRID_W - (WIN_W - 1), axis=1, stride=1, stride_axis=0)
        tiles.append(jnp.where(in_window, jnp.where(left, t_left, t_right), MASKED))
    for e in range(N_FULL):
        tbl_ref[head, e] = jnp.where(left, tiles[e + 1], tiles[e])
    tbl_ref[head, E_RIGHT_MASKED] = jnp.where(left, tiles[WIN_H - 1 - WIN_H // 2], MASKED)
    tbl_ref[head, E_LEFT_MASKED] = jnp.where(left, MASKED, tiles[WIN_H - 1 + WIN_H // 2 - 1])


def _reversed_padded(rpb):
    _, nr, nc = rpb.shape
    return jnp.pad(rpb[:, :, ::-1], ((0, 0), (0, 16 - nr), (0, LANES - nc)))


def _window_rows(case, j):
    offset = (0, WIN_H // 2, WIN_H)[case]

    def in_window(rq, rk):
        first = (0, rq, WIN_H // 2)[case]
        return first <= rk < first + WIN_H

    out = []
    for rk in range(N_KEY_BLOCKS * KEY_BLOCK_ROWS):
        left_ok, right_ok = in_window(2 * j, rk), in_window(2 * j + 1, rk)
        dr_left = rk - 2 * j - offset
        if left_ok and right_ok:
            out.append((rk, dr_left + WIN_H - 2))
        elif left_ok:
            assert dr_left == -WIN_H // 2
            out.append((rk, E_RIGHT_MASKED))
        elif right_ok:
            assert dr_left - 1 == WIN_H // 2 - 1
            out.append((rk, E_LEFT_MASKED))
    return out


def _step_plan(step, steps_per_seq):
    blocks_per_seq = steps_per_seq * SUB_BLOCKS
    step_first = min(max(SUB_BLOCKS * step - 1, 0), blocks_per_seq - STEP_KEY_BLOCKS)
    plan = []
    for sub in range(SUB_BLOCKS):
        qb = SUB_BLOCKS * step + sub
        case = 0 if qb == 0 else 2 if qb == blocks_per_seq - 1 else 1
        plan.append((case, min(max(qb - 1, 0), blocks_per_seq - N_KEY_BLOCKS) - step_first))
    return plan


def _attn_kernel(q_ref, *rest, steps_per_seq, n_steps):
    k_refs, v_refs = rest[:STEP_KEY_BLOCKS], rest[STEP_KEY_BLOCKS:2 * STEP_KEY_BLOCKS]
    rpb_ref, x_ref, wo_ref, wup_ref, wdn_ref, o_ref, wup_o, wdn_o, obuf, wo_bf, tbl_ref = rest[2 * STEP_KEY_BLOCKS:]
    mq = Q_ROWS * GRID_W
    t = pl.program_id(0)
    wup_o[...] = wup_ref[...].astype(BF16)
    wdn_o[...] = wdn_ref[...].astype(BF16)
    step = lax.rem(t, steps_per_seq)
    slot = lax.rem(t, 2)

    @pl.when(t == 0)
    def _():
        wo_bf[...] = wo_ref[...].astype(BF16)
        obuf[1] = jnp.zeros(obuf.shape[1:], BF16)

        def fill(head, carry):
            _fill_bias_table(rpb_ref, tbl_ref, head)
            return carry

        lax.fori_loop(0, N_HEADS, fill, 0)

    last = steps_per_seq - 1
    n_key_rows = N_KEY_BLOCKS * KEY_BLOCK_ROWS
    left = lax.broadcasted_iota(jnp.int32, (mq, LANES), 1) < HEAD_DIM
    ones_rows = jnp.ones((ONES_ROWS, n_key_rows * GRID_W), BF16)

    def project_previous(part=None):
        a = jnp.concatenate([obuf[1 - slot, p] for p in range(N_PAIRS)], axis=1)
        cols = slice(None) if part is None else slice(part * PROJ_CHUNK, (part + 1) * PROJ_CHUNK)
        o_ref[:, cols] = x_ref[:, cols] + jnp.dot(a, wo_bf[:, cols], preferred_element_type=F32)

    def scores(p, sub, lo):
        q = q_ref[p, sub * mq:(sub + 1) * mq]
        q2 = jnp.concatenate([jnp.where(left, q, 0), jnp.where(left, 0, q)], axis=0)
        k = jnp.concatenate([k_refs[lo + i][p] for i in range(N_KEY_BLOCKS)], axis=0)
        return lax.dot_general(k, q2, (((1,), (1,)), ((), ())), preferred_element_type=F32)

    def finish(case, p, sub, lo, st):
        vt = jnp.concatenate([v_refs[lo + i][p] for i in range(N_KEY_BLOCKS)], axis=1)
        outs = []
        for hh in range(2):
            head = 2 * p + hh
            cols = []
            for j in range(Q_ROWS // 2):
                lt = 2 * hh + j
                rows = _window_rows(case, j)
                sc = [st[rk * GRID_W:(rk + 1) * GRID_W, lt * LANES:(lt + 1) * LANES] + tbl_ref[head, e]
                      for rk, e in rows]
                m = jnp.max(functools.reduce(jnp.maximum, sc), axis=0, keepdims=True)
                by_row = {rk: jnp.exp(x - m).astype(BF16) for (rk, _), x in zip(rows, sc)}
                cols.append(jnp.concatenate(
                    [by_row[rk] if rk in by_row else jnp.zeros((GRID_W, LANES), BF16)
                     for rk in range(n_key_rows)], axis=0))
            probs_t = jnp.concatenate(cols, axis=1)
            v_ones = jnp.concatenate([vt[hh * HEAD_DIM:(hh + 1) * HEAD_DIM], ones_rows], axis=0)
            o_t = jnp.dot(v_ones, probs_t, preferred_element_type=F32)
            outs.append(o_t[:HEAD_DIM] / o_t[HEAD_DIM:HEAD_DIM + 1])
        obuf[slot, p, sub * mq:(sub + 1) * mq] = jnp.concatenate(outs, axis=0).T.astype(BF16)

    assert all(_step_plan(s, steps_per_seq) == _step_plan(1, steps_per_seq) for s in range(1, last))
    for rep, cond in ((0, step == 0), (1, (step > 0) & (step < last)), (last, step == last)):
        @pl.when(cond & (t < n_steps))
        def _(plan=_step_plan(rep, steps_per_seq)):
            units = [(p, sub) for p in range(N_PAIRS) for sub in range(SUB_BLOCKS)]
            every = len(units) * PROJ_CHUNK // (N_HEADS * HEAD_DIM)
            st = scores(*units[0], plan[units[0][1]][1])
            for u, (p, sub) in enumerate(units):
                st_next = None
                if u + 1 < len(units):
                    p_next, sub_next = units[u + 1]
                    st_next = scores(p_next, sub_next, plan[sub_next][1])
                if u % every == every - 1:
                    project_previous(u // every)
                finish(plan[sub][0], p, sub, plan[sub][1], st)
                st = st_next

    pl.when(t == n_steps)(project_previous)


def _attn(qk, vt, rpb, x, wo, wup, wdn, seq):
    n, d = x.shape
    mq = SUB_BLOCKS * Q_ROWS * GRID_W
    kb = KEY_BLOCK
    steps = seq // mq
    n_steps = n // mq
    last_start = seq // kb - STEP_KEY_BLOCKS

    def first_key_block(t):
        blk = jnp.minimum(t, n_steps - 1)
        return (blk // steps) * (seq // kb) + jnp.clip(SUB_BLOCKS * (blk % steps) - 1, 0, last_start)

    tile = pl.BlockSpec((mq, d), lambda t: (jnp.maximum(t - 1, 0), 0))
    in_specs = [pl.BlockSpec((None, N_PAIRS, mq, LANES), lambda t: (0, 0, jnp.minimum(t, n_steps - 1), 0))]
    in_specs += [pl.BlockSpec((None, N_PAIRS, kb, LANES), lambda t, j=j: (1, 0, first_key_block(t) + j, 0))
                 for j in range(STEP_KEY_BLOCKS)]
    in_specs += [pl.BlockSpec((None, N_PAIRS, LANES, kb), lambda t, j=j: (first_key_block(t) + j, 0, 0, 0))
                 for j in range(STEP_KEY_BLOCKS)]
    cast_in, cast_out, cast_shape = _cast_specs(wup, wdn, 1, n_steps, lambda t: jnp.minimum(t, n_steps - 1))
    in_specs += [_resident(rpb.shape), tile, _resident((d, d))] + cast_in
    return pl.pallas_call(
        functools.partial(_attn_kernel, steps_per_seq=steps, n_steps=n_steps),
        grid=(n_steps + 1,),
        in_specs=in_specs,
        out_specs=[tile] + cast_out,
        out_shape=[jax.ShapeDtypeStruct((n, d), F32)] + cast_shape,
        scratch_shapes=[pltpu.VMEM((2, N_PAIRS, mq, LANES), BF16), pltpu.VMEM((d, d), BF16),
                        pltpu.VMEM((N_HEADS, N_ENTRIES, GRID_W, LANES), F32)],
        compiler_params=_params(),
        name="natten",
    )(*([qk] * (1 + STEP_KEY_BLOCKS)), *([vt] * STEP_KEY_BLOCKS), rpb, x, wo, wup, wdn)


def kernel(x, norm_mix, conv_w_in, conv_w, conv_w_out, attn_w_qkv, attn_rpb, attn_w_o, norm_mlp, mlp_w_up,
           mlp_w_down, norm_final):
    batch, seq, d = x.shape
    n = batch * seq
    assert d == N_HEADS * HEAD_DIM and seq % (GRID_W * Q_ROWS * SUB_BLOCKS) == 0 and d % PROJ_CHUNK == 0
    assert seq // KEY_BLOCK >= STEP_KEY_BLOCKS
    assert seq % ROW_TILE == 0 and ROW_TILE % HALO == 0 and d % CONV_CHUNK == 0
    assert N_KEY_BLOCKS == 3 and Q_ROWS == KEY_BLOCK_ROWS == WIN_H // 2 and ROW_TILE % KEY_BLOCK == 0
    assert norm_mix.shape[0] == 2 and conv_w_in.shape[0] == 1 and attn_w_qkv.shape[0] == 1

    xf = x.reshape(n, d)
    xf, wup, wdn = _conv_mixer(xf, norm_mix[0:1], conv_w_in[0], conv_w[0], conv_w_out[0], mlp_w_up, mlp_w_down, seq)
    xf = _mlp(xf, norm_mlp[0:1], wup, wdn)

    qk, vt = _qkv(xf, norm_mix[1:2], attn_w_qkv[0])
    xf, wup, wdn = _attn(qk, vt, _reversed_padded(attn_rpb[0]), xf, attn_w_o[0], mlp_w_up, mlp_w_down, seq)
    out = _mlp(xf, norm_mlp[1:2], wup, wdn, norm_final.reshape(1, d))
    return out.reshape(batch, seq, d)
```

```python
import functools

import jax
import jax.numpy as jnp
from jax import lax
from jax.experimental import pallas as pl
from jax.experimental.pallas import tpu as pltpu

F32 = jnp.float32
BF16 = jnp.bfloat16

NORM_EPS = 1e-6
N_HEADS = 16
HEAD_DIM = 64
GRID_W = 64
WIN_H = 8
WIN_W = 16
LANES = 128
N_PAIRS = N_HEADS * HEAD_DIM // LANES

ROW_TILE = 1024
FF_CHUNK = 512
MLP_SUBTILES = 2
HALO = 16
CONV_CHUNK = 256
Q_ROWS = 4
KEY_BLOCK_ROWS = 4
KEY_BLOCK = KEY_BLOCK_ROWS * GRID_W
N_KEY_BLOCKS = Q_ROWS // KEY_BLOCK_ROWS + WIN_H // KEY_BLOCK_ROWS
SUB_BLOCKS = 2
STEP_KEY_BLOCKS = N_KEY_BLOCKS + SUB_BLOCKS - 1
MASKED = -1e30
PROJ_CHUNK = 256

N_FULL = 2 * WIN_H - 2
E_RIGHT_MASKED = N_FULL
E_LEFT_MASKED = N_FULL + 1
N_ENTRIES = N_FULL + 2
ONES_ROWS = 16

VMEM_LIMIT = 52 * 1024 * 1024


def _params():
    return pltpu.CompilerParams(dimension_semantics=("arbitrary",), vmem_limit_bytes=VMEM_LIMIT)


def _resident(shape):
    return pl.BlockSpec(shape, lambda *_: (0,) * len(shape), pipeline_mode=pl.Buffered(1))


def _rmsnorm(x, g):
    ms = jnp.mean(x * x, axis=-1, keepdims=True)
    return x * lax.rsqrt(ms + NORM_EPS) * g


def _cast_specs(wup, wdn, layer, n_slabs, slab_of):
    _, d, ff = wup.shape
    in_specs = [pl.BlockSpec((None, d // n_slabs, ff), lambda s: (layer, slab_of(s), 0)),
                pl.BlockSpec((None, ff // n_slabs, d), lambda s: (layer, slab_of(s), 0))]
    out_specs = [pl.BlockSpec((d // n_slabs, ff), lambda s: (slab_of(s), 0)),
                 pl.BlockSpec((ff // n_slabs, d), lambda s: (slab_of(s), 0))]
    out_shape = [jax.ShapeDtypeStruct((d, ff), BF16), jax.ShapeDtypeStruct((ff, d), BF16)]
    return in_specs, out_specs, out_shape


def _conv_kernel(x_ref, xprev_ref, xnext_ref, g_ref, win_ref, cw_ref, wout_ref, wup_ref, wdn_ref, o_ref, wup_o,
                 wdn_o, hbuf, ybuf, win_bf, wout_bf, *, tiles_per_seq):
    tm, d = x_ref.shape
    rows = tm + 2 * HALO
    step = pl.program_id(0)
    wup_o[...] = wup_ref[...].astype(BF16)
    wdn_o[...] = wdn_ref[...].astype(BF16)

    @pl.when(step == 0)
    def _():
        win_bf[...] = win_ref[...].astype(BF16)
        wout_bf[...] = wout_ref[...].astype(BF16)

    pos = lax.rem(step, tiles_per_seq)
    g = g_ref[...]
    hbuf[0:HALO] = _rmsnorm(xprev_ref[...], g).astype(BF16)
    hbuf[HALO:HALO + tm] = _rmsnorm(x_ref[...], g).astype(BF16)
    hbuf[HALO + tm:rows] = _rmsnorm(xnext_ref[...], g).astype(BF16)
    h = hbuf[...]
    row = lax.broadcasted_iota(jnp.int32, (rows, 1), 0)
    outside = ((row < HALO) & (pos == 0)) | ((row >= HALO + tm) & (pos == tiles_per_seq - 1))
    for j in range(d // CONV_CHUNK):
        cols = slice(j * CONV_CHUNK, (j + 1) * CONV_CHUNK)
        c = jnp.dot(h, win_bf[:, d + j * CONV_CHUNK:d + (j + 1) * CONV_CHUNK], preferred_element_type=F32)
        v = jnp.dot(h, win_bf[:, 2 * d + j * CONV_CHUNK:2 * d + (j + 1) * CONV_CHUNK], preferred_element_type=F32)
        z = jnp.where(outside, 0.0, c * v)
        z_m1 = pltpu.roll(z, 1, axis=0)[HALO:HALO + tm]
        z_p1 = pltpu.roll(z, rows - 1, axis=0)[HALO:HALO + tm]
        zc = cw_ref[0:1, cols] * z_m1 + cw_ref[1:2, cols] * z[HALO:HALO + tm] + cw_ref[2:3, cols] * z_p1
        gate = jnp.dot(hbuf[HALO:HALO + tm], win_bf[:, cols], preferred_element_type=F32)
        ybuf[:, cols] = (gate * zc).astype(BF16)
    o_ref[...] = x_ref[...] + jnp.dot(ybuf[...], wout_bf[...], preferred_element_type=F32)


def _conv_mixer(x, g, w_in, cw, w_out, wup, wdn, seq):
    n, d = x.shape
    cast_in, cast_out, cast_shape = _cast_specs(wup, wdn, 0, n // ROW_TILE, lambda i: i)
    per = ROW_TILE // HALO
    tile = pl.BlockSpec((ROW_TILE, d), lambda i: (i, 0))
    prev_spec = pl.BlockSpec((HALO, d), lambda i: (jnp.maximum(i * per - 1, 0), 0))
    next_spec = pl.BlockSpec((HALO, d), lambda i: (jnp.minimum((i + 1) * per, n // HALO - 1), 0))
    return pl.pallas_call(
        functools.partial(_conv_kernel, tiles_per_seq=seq // ROW_TILE),
        grid=(n // ROW_TILE,),
        in_specs=[tile, prev_spec, next_spec, _resident((1, d)), _resident((d, 3 * d)), _resident(cw.shape),
                  _resident((d, d))] + cast_in,
        out_specs=[tile] + cast_out,
        out_shape=[jax.ShapeDtypeStruct((n, d), F32)] + cast_shape,
        scratch_shapes=[pltpu.VMEM((ROW_TILE + 2 * HALO, d), BF16), pltpu.VMEM((ROW_TILE, d), BF16),
                        pltpu.VMEM((d, 3 * d), BF16), pltpu.VMEM((d, d), BF16)],
        compiler_params=_params(),
        name="conv_mixer",
    )(x, x, x, g, w_in, cw, w_out, wup, wdn)


def _mlp_kernel(x_ref, g_ref, wup_ref, wdn_ref, *rest, final_norm):
    o_ref = rest[-1]
    ff = wup_ref.shape[1]
    sub = x_ref.shape[0] // MLP_SUBTILES
    for r in range(MLP_SUBTILES):
        rows = slice(r * sub, (r + 1) * sub)
        x = x_ref[rows, :]
        h = _rmsnorm(x, g_ref[...]).astype(BF16)
        acc = x
        for c in range(ff // FF_CHUNK):
            cols = slice(c * FF_CHUNK, (c + 1) * FF_CHUNK)
            a = jnp.maximum(jnp.dot(h, wup_ref[:, cols], preferred_element_type=F32), 0.0)
            acc = acc + jnp.dot((a * a).astype(BF16), wdn_ref[cols, :], preferred_element_type=F32)
        if final_norm:
            acc = _rmsnorm(acc, rest[0][...])
        o_ref[rows, :] = acc


def _mlp(x, g, wup, wdn, g_final=None):
    n, d = x.shape
    ff = wup.shape[1]
    tile = pl.BlockSpec((ROW_TILE, d), lambda i: (i, 0))
    in_specs = [tile, _resident((1, d)), _resident((d, ff)), _resident((ff, d))]
    args = [x, g, wup, wdn]
    if g_final is not None:
        in_specs.append(_resident((1, d)))
        args.append(g_final)
    return pl.pallas_call(
        functools.partial(_mlp_kernel, final_norm=g_final is not None),
        grid=(n // ROW_TILE,),
        in_specs=in_specs,
        out_specs=tile,
        out_shape=jax.ShapeDtypeStruct((n, d), F32),
        compiler_params=_params(),
        name="mlp_final" if g_final is not None else "mlp",
    )(*args)


def _qkv_kernel(x_ref, g_ref, w_ref, qk_ref, vt_ref, w_bf):
    d = x_ref.shape[1]

    @pl.when(pl.program_id(0) == 0)
    def _():
        w_bf[...] = w_ref[...].astype(BF16)

    h = _rmsnorm(x_ref[...], g_ref[...]).astype(BF16)
    for t in range(3):
        u = jnp.dot(h, w_bf[:, t * d:(t + 1) * d], preferred_element_type=F32)
        if t == 0:
            u = u * (HEAD_DIM ** -0.5)
        for p in range(N_PAIRS):
            blk = u[:, p * LANES:(p + 1) * LANES]
            if t < 2:
                qk_ref[t, p] = blk.astype(BF16)
            else:
                for b in range(vt_ref.shape[0]):
                    vt_ref[b, p] = blk[b * KEY_BLOCK:(b + 1) * KEY_BLOCK].T.astype(BF16)


def _qkv(x, g, w):
    n, d = x.shape
    return pl.pallas_call(
        _qkv_kernel,
        grid=(n // ROW_TILE,),
        in_specs=[pl.BlockSpec((ROW_TILE, d), lambda i: (i, 0)), _resident((1, d)), _resident((d, 3 * d))],
        out_specs=[pl.BlockSpec((2, N_PAIRS, ROW_TILE, LANES), lambda i: (0, 0, i, 0)),
                   pl.BlockSpec((ROW_TILE // KEY_BLOCK, N_PAIRS, LANES, KEY_BLOCK), lambda i: (i, 0, 0, 0))],
        out_shape=[jax.ShapeDtypeStruct((2, N_PAIRS, n, LANES), BF16),
                   jax.ShapeDtypeStruct((n // KEY_BLOCK, N_PAIRS, LANES, KEY_BLOCK), BF16)],
        scratch_shapes=[pltpu.VMEM((d, 3 * d), BF16)],
        compiler_params=_params(),
        name="qkv",
    )(x, g, w)


def _bias_table_kernel(rpb_ref, o_ref):
    shape = (GRID_W, LANES)
    kc = lax.broadcasted_iota(jnp.int32, shape, 0)
    lane = lax.broadcasted_iota(jnp.int32, shape, 1)
    c = lane & (GRID_W - 1)
    col_start = jnp.clip(c - WIN_W // 2, 0, GRID_W - WIN_W)
    in_window = (kc >= col_start) & (kc < col_start + WIN_W)
    left = lane < GRID_W
    tiles = []
    for r in range(2 * WIN_H - 1):
        row = jnp.broadcast_to(rpb_ref[0, r:r + 1, :], shape)
        t_left = pltpu.roll(row, LANES - (WIN_W - 1), axis=1, stride=1, stride_axis=0)
        t_right = pltpu.roll(row, GRID_W - (WIN_W - 1), axis=1, stride=1, stride_axis=0)
        tiles.append(jnp.where(in_window, jnp.where(left, t_left, t_right), MASKED))
    for e in range(N_FULL):
        o_ref[0, e] = jnp.where(left, tiles[e + 1], tiles[e])
    o_ref[0, E_RIGHT_MASKED] = jnp.where(left, tiles[WIN_H - 1 - WIN_H // 2], MASKED)
    o_ref[0, E_LEFT_MASKED] = jnp.where(left, MASKED, tiles[WIN_H - 1 + WIN_H // 2 - 1])


def _bias_table(rpb):
    h, nr, nc = rpb.shape
    padded = jnp.pad(rpb[:, :, ::-1], ((0, 0), (0, 16 - nr), (0, LANES - nc)))
    return pl.pallas_call(
        _bias_table_kernel,
        grid=(h,),
        in_specs=[pl.BlockSpec((1, 16, LANES), lambda i: (i, 0, 0))],
        out_specs=pl.BlockSpec((1, N_ENTRIES, GRID_W, LANES), lambda i: (i, 0, 0, 0)),
        out_shape=jax.ShapeDtypeStruct((h, N_ENTRIES, GRID_W, LANES), F32),
        compiler_params=_params(),
        name="bias_table",
    )(padded)


def _window_rows(case, j):
    offset = (0, WIN_H // 2, WIN_H)[case]

    def in_window(rq, rk):
        first = (0, rq, WIN_H // 2)[case]
        return first <= rk < first + WIN_H

    out = []
    for rk in range(N_KEY_BLOCKS * KEY_BLOCK_ROWS):
        left_ok, right_ok = in_window(2 * j, rk), in_window(2 * j + 1, rk)
        dr_left = rk - 2 * j - offset
        if left_ok and right_ok:
            out.append((rk, dr_left + WIN_H - 2))
        elif left_ok:
            assert dr_left == -WIN_H // 2
            out.append((rk, E_RIGHT_MASKED))
        elif right_ok:
            assert dr_left - 1 == WIN_H // 2 - 1
            out.append((rk, E_LEFT_MASKED))
    return out


def _step_plan(step, steps_per_seq):
    blocks_per_seq = steps_per_seq * SUB_BLOCKS
    step_first = min(max(SUB_BLOCKS * step - 1, 0), blocks_per_seq - STEP_KEY_BLOCKS)
    plan = []
    for sub in range(SUB_BLOCKS):
        qb = SUB_BLOCKS * step + sub
        case = 0 if qb == 0 else 2 if qb == blocks_per_seq - 1 else 1
        plan.append((case, min(max(qb - 1, 0), blocks_per_seq - N_KEY_BLOCKS) - step_first))
    return plan


def _attn_kernel(q_ref, *rest, steps_per_seq, n_steps):
    k_refs, v_refs = rest[:STEP_KEY_BLOCKS], rest[STEP_KEY_BLOCKS:2 * STEP_KEY_BLOCKS]
    tbl_ref, x_ref, wo_ref, wup_ref, wdn_ref, o_ref, wup_o, wdn_o, obuf, wo_bf = rest[2 * STEP_KEY_BLOCKS:]
    mq = Q_ROWS * GRID_W
    t = pl.program_id(0)
    wup_o[...] = wup_ref[...].astype(BF16)
    wdn_o[...] = wdn_ref[...].astype(BF16)
    step = lax.rem(t, steps_per_seq)
    slot = lax.rem(t, 2)

    @pl.when(t == 0)
    def _():
        wo_bf[...] = wo_ref[...].astype(BF16)
        obuf[1] = jnp.zeros(obuf.shape[1:], BF16)

    last = steps_per_seq - 1
    n_key_rows = N_KEY_BLOCKS * KEY_BLOCK_ROWS
    left = lax.broadcasted_iota(jnp.int32, (mq, LANES), 1) < HEAD_DIM
    ones_rows = jnp.ones((ONES_ROWS, n_key_rows * GRID_W), BF16)

    def project_previous(part=None):
        a = jnp.concatenate([obuf[1 - slot, p] for p in range(N_PAIRS)], axis=1)
        cols = slice(None) if part is None else slice(part * PROJ_CHUNK, (part + 1) * PROJ_CHUNK)
        o_ref[:, cols] = x_ref[:, cols] + jnp.dot(a, wo_bf[:, cols], preferred_element_type=F32)

    def scores(p, sub, lo):
        q = q_ref[p, sub * mq:(sub + 1) * mq]
        q2 = jnp.concatenate([jnp.where(left, q, 0), jnp.where(left, 0, q)], axis=0)
        k = jnp.concatenate([k_refs[lo + i][p] for i in range(N_KEY_BLOCKS)], axis=0)
        return lax.dot_general(k, q2, (((1,), (1,)), ((), ())), preferred_element_type=F32)

    def finish(case, p, sub, lo, st):
        vt = jnp.concatenate([v_refs[lo + i][p] for i in range(N_KEY_BLOCKS)], axis=1)
        outs = []
        for hh in range(2):
            head = 2 * p + hh
            cols = []
            for j in range(Q_ROWS // 2):
                lt = 2 * hh + j
                rows = _window_rows(case, j)
                sc = [st[rk * GRID_W:(rk + 1) * GRID_W, lt * LANES:(lt + 1) * LANES] + tbl_ref[head, e]
                      for rk, e in rows]
                m = jnp.max(functools.reduce(jnp.maximum, sc), axis=0, keepdims=True)
                by_row = {rk: jnp.exp(x - m).astype(BF16) for (rk, _), x in zip(rows, sc)}
                cols.append(jnp.concatenate(
                    [by_row[rk] if rk in by_row else jnp.zeros((GRID_W, LANES), BF16)
                     for rk in range(n_key_rows)], axis=0))
            probs_t = jnp.concatenate(cols, axis=1)
            v_ones = jnp.concatenate([vt[hh * HEAD_DIM:(hh + 1) * HEAD_DIM], ones_rows], axis=0)
            o_t = jnp.dot(v_ones, probs_t, preferred_element_type=F32)
            outs.append(o_t[:HEAD_DIM] / o_t[HEAD_DIM:HEAD_DIM + 1])
        obuf[slot, p, sub * mq:(sub + 1) * mq] = jnp.concatenate(outs, axis=0).T.astype(BF16)

    assert all(_step_plan(s, steps_per_seq) == _step_plan(1, steps_per_seq) for s in range(1, last))
    for rep, cond in ((0, step == 0), (1, (step > 0) & (step < last)), (last, step == last)):
        @pl.when(cond & (t < n_steps))
        def _(plan=_step_plan(rep, steps_per_seq)):
            units = [(p, sub) for p in range(N_PAIRS) for sub in range(SUB_BLOCKS)]
            every = len(units) * PROJ_CHUNK // (N_HEADS * HEAD_DIM)
            st = scores(*units[0], plan[units[0][1]][1])
            for u, (p, sub) in enumerate(units):
                st_next = None
                if u + 1 < len(units):
                    p_next, sub_next = units[u + 1]
                    st_next = scores(p_next, sub_next, plan[sub_next][1])
                if u % every == every - 1:
                    project_previous(u // every)
                finish(plan[sub][0], p, sub, plan[sub][1], st)
                st = st_next

    pl.when(t == n_steps)(project_previous)


def _attn(qk, vt, tbl, x, wo, wup, wdn, seq):
    n, d = x.shape
    mq = SUB_BLOCKS * Q_ROWS * GRID_W
    kb = KEY_BLOCK
    steps = seq // mq
    n_steps = n // mq
    last_start = seq // kb - STEP_KEY_BLOCKS

    def first_key_block(t):
        blk = jnp.minimum(t, n_steps - 1)
        return (blk // steps) * (seq // kb) + jnp.clip(SUB_BLOCKS * (blk % steps) - 1, 0, last_start)

    tile = pl.BlockSpec((mq, d), lambda t: (jnp.maximum(t - 1, 0), 0))
    in_specs = [pl.BlockSpec((None, N_PAIRS, mq, LANES), lambda t: (0, 0, jnp.minimum(t, n_steps - 1), 0))]
    in_specs += [pl.BlockSpec((None, N_PAIRS, kb, LANES), lambda t, j=j: (1, 0, first_key_block(t) + j, 0))
                 for j in range(STEP_KEY_BLOCKS)]
    in_specs += [pl.BlockSpec((None, N_PAIRS, LANES, kb), lambda t, j=j: (first_key_block(t) + j, 0, 0, 0))
                 for j in range(STEP_KEY_BLOCKS)]
    cast_in, cast_out, cast_shape = _cast_specs(wup, wdn, 1, n_steps, lambda t: jnp.minimum(t, n_steps - 1))
    in_specs += [_resident(tbl.shape), tile, _resident((d, d))] + cast_in
    return pl.pallas_call(
        functools.partial(_attn_kernel, steps_per_seq=steps, n_steps=n_steps),
        grid=(n_steps + 1,),
        in_specs=in_specs,
        out_specs=[tile] + cast_out,
        out_shape=[jax.ShapeDtypeStruct((n, d), F32)] + cast_shape,
        scratch_shapes=[pltpu.VMEM((2, N_PAIRS, mq, LANES), BF16), pltpu.VMEM((d, d), BF16)],
        compiler_params=_params(),
        name="natten",
    )(*([qk] * (1 + STEP_KEY_BLOCKS)), *([vt] * STEP_KEY_BLOCKS), tbl, x, wo, wup, wdn)


def kernel(x, norm_mix, conv_w_in, conv_w, conv_w_out, attn_w_qkv, attn_rpb, attn_w_o, norm_mlp, mlp_w_up,
           mlp_w_down, norm_final):
    batch, seq, d = x.shape
    n = batch * seq
    assert d == N_HEADS * HEAD_DIM and seq % (GRID_W * Q_ROWS * SUB_BLOCKS) == 0 and d % PROJ_CHUNK == 0
    assert seq // KEY_BLOCK >= STEP_KEY_BLOCKS
    assert seq % ROW_TILE == 0 and ROW_TILE % HALO == 0 and d % CONV_CHUNK == 0
    assert N_KEY_BLOCKS == 3 and Q_ROWS == KEY_BLOCK_ROWS == WIN_H // 2 and ROW_TILE % KEY_BLOCK == 0
    assert norm_mix.shape[0] == 2 and conv_w_in.shape[0] == 1 and attn_w_qkv.shape[0] == 1

    xf = x.reshape(n, d)
    xf, wup, wdn = _conv_mixer(xf, norm_mix[0:1], conv_w_in[0], conv_w[0], conv_w_out[0], mlp_w_up, mlp_w_down, seq)
    xf = _mlp(xf, norm_mlp[0:1], wup, wdn)

    qk, vt = _qkv(xf, norm_mix[1:2], attn_w_qkv[0])
    tbl = _bias_table(attn_rpb[0])
    xf, wup, wdn = _attn(qk, vt, tbl, xf, attn_w_o[0], mlp_w_up, mlp_w_down, seq)
    out = _mlp(xf, norm_mlp[1:2], wup, wdn, norm_final.reshape(1, d))
    return out.reshape(batch, seq, d)
```
